```python
import math
import jax, jax.numpy as jnp
from jax import lax
import numpy as np

D_MODEL = 2048
BATCH = 1
SEQ = 16384
DEPTH = 2

CHUNK = 64
QBLK = 128
D_MIX = D_MODEL

A_HEADS = 6
A_DHALF = 64
A_DV = 2 * A_DHALF
A_WIDTH = A_HEADS * A_DV

B_WIDTH = 512
CONV_W = 3

C_HEADS = 6
C_NOPE = 64
C_ROPE = 32
C_DV = 128
C_WIDTH = C_HEADS * C_DV
C_Q_RANK = 512
C_KV_RANK = 256
ROPE_THETA = 10000.0

D_FF = 5632

ALPHA = (2.0 * DEPTH) ** 0.25
BETA = (8.0 * DEPTH) ** -0.25

LN_EPS = 1e-5
RMS_EPS = 1e-6
NEG = -1e30

A_Q_COLS = A_HEADS * 2 * A_DHALF
A_K_COLS = A_HEADS * 2 * A_DHALF
A_V_COLS = A_HEADS * A_DV
IN_SIZES = (A_Q_COLS, A_K_COLS, A_V_COLS, B_WIDTH, B_WIDTH, B_WIDTH, C_Q_RANK, C_KV_RANK, C_ROPE)
IN_SPLITS = (768, 1536, 2304, 2816, 3328, 3840, 4352, 4608)
P_IN = 4640

kernel_name = "hymba_style_diffattn_shortconv_mla_convffn_deepnorm"


def layer_norm(x, g, b):
    xf = x.astype(jnp.float32)
    mu = xf.mean(-1, keepdims=True)
    var = jnp.square(xf - mu).mean(-1, keepdims=True)
    return ((xf - mu) * lax.rsqrt(var + LN_EPS) * g.astype(jnp.float32) + b.astype(jnp.float32)).astype(x.dtype)


def rms_norm(x, g):
    xf = x.astype(jnp.float32)
    r = lax.rsqrt(jnp.mean(jnp.square(xf), -1, keepdims=True) + RMS_EPS)
    return (xf * r * g.astype(jnp.float32)).astype(x.dtype)


def causal_dwconv3(x, w):
    s = x.shape[1]
    xp = jnp.pad(x, ((0, 0), (CONV_W - 1, 0), (0, 0)))
    return sum(w[i] * xp[:, i:i + s] for i in range(CONV_W))


def rope(x, pos):
    d = x.shape[-1]
    inv_freq = ROPE_THETA ** (-jnp.arange(0, d, 2, dtype=jnp.float32) / d)
    ang = pos.astype(jnp.float32)[..., None] * inv_freq
    cos = jnp.cos(ang)[:, :, None, :]
    sin = jnp.sin(ang)[:, :, None, :]
    xf = x.astype(jnp.float32)
    x1, x2 = xf[..., : d // 2], xf[..., d // 2:]
    return jnp.concatenate([x1 * cos - x2 * sin, x1 * sin + x2 * cos], -1).astype(x.dtype)


def chunk_mask(q_idx, s):
    return (q_idx[:, None] // CHUNK) >= (jnp.arange(s)[None, :] // CHUNK)


def to_qblocks(a):
    b, h, s = a.shape[:3]
    return jnp.moveaxis(a.reshape((b, h, s // QBLK, QBLK) + a.shape[3:]), 2, 0)


def from_qblocks(a):
    nq, b, h, q, d = a.shape
    return jnp.moveaxis(a, 0, 2).reshape(b, h, nq * q, d)


def diff_attention(q, k, v, lam, pos, slopes):
    b, h, s = q.shape[:3]
    nq = s // QBLK
    scale = A_DHALF ** -0.5
    k1, k2 = k[..., 0, :], k[..., 1, :]
    pos_k = pos[:, None, None, :]
    qb = to_qblocks(q)
    pos_q = jnp.moveaxis(pos.reshape(b, nq, QBLK), 1, 0)
    idx_q = jnp.arange(s).reshape(nq, QBLK)

    def one(blk):
        qblk, pq, iq = blk
        dist = jnp.abs(pq[:, None, :, None] - pos_k).astype(jnp.float32)
        bias = -slopes[None, :, None, None] * dist
        mask = chunk_mask(iq, s)

        def probs(qh, kh):
            sc = jnp.einsum('bhqd,bhkd->bhqk', qh, kh).astype(jnp.float32) * scale + bias
            return jax.nn.softmax(jnp.where(mask, sc, NEG), axis=-1)

        p = probs(qblk[..., 0, :], k1) - lam * probs(qblk[..., 1, :], k2)
        return jnp.einsum('bhqk,bhkd->bhqd', p.astype(v.dtype), v)

    return from_qblocks(lax.map(one, (qb, pos_q, idx_q)))


def mla_attention(q, k, v):
    b, h, s = q.shape[:3]
    nq = s // QBLK
    scale = (C_NOPE + C_ROPE) ** -0.5
    qb = to_qblocks(q)
    idx_q = jnp.arange(s).reshape(nq, QBLK)

    def one(blk):
        qblk, iq = blk
        sc = jnp.einsum('bhqd,bhkd->bhqk', qblk, k).astype(jnp.float32) * scale
        p = jax.nn.softmax(jnp.where(chunk_mask(iq, s), sc, NEG), axis=-1)
        return jnp.einsum('bhqk,bhkd->bhqd', p.astype(v.dtype), v)

    return from_qblocks(lax.map(one, (qb, idx_q)))


def mixer(h, pos, lam_init, w_in, diff_lambda, diff_norm_g, conv_w,
          mla_q_norm_g, mla_kv_norm_g, w_uq, w_ukv, w_o):
    b, s, _ = h.shape
    proj = h @ w_in
    a_q, a_k, a_v, b_h, b_b, b_c, c_q, c_kv, c_kr = jnp.split(proj, IN_SPLITS, axis=-1)

    q = a_q.reshape(b, s, A_HEADS, 2, A_DHALF).transpose(0, 2, 1, 3, 4)
    k = a_k.reshape(b, s, A_HEADS, 2, A_DHALF).transpose(0, 2, 1, 3, 4)
    v = a_v.reshape(b, s, A_HEADS, A_DV).transpose(0, 2, 1, 3)
    lf = diff_lambda.astype(jnp.float32)
    lam = jnp.exp(jnp.sum(lf[0] * lf[1])) - jnp.exp(jnp.sum(lf[2] * lf[3])) + lam_init
    slopes = 2.0 ** (-8.0 * (jnp.arange(A_HEADS, dtype=jnp.float32) + 1.0) / A_HEADS)
    o_a = diff_attention(q, k, v, lam, pos, slopes)
    o_a = rms_norm(o_a, diff_norm_g) * (1.0 - lam_init)
    o_a = o_a.transpose(0, 2, 1, 3).reshape(b, s, A_WIDTH)

    o_b = b_b * causal_dwconv3(b_c * b_h, conv_w)

    cq = rms_norm(c_q, mla_q_norm_g)
    qc = (cq @ w_uq).reshape(b, s, C_HEADS, C_NOPE + C_ROPE)
    q_nope, q_pe = qc[..., :C_NOPE], rope(qc[..., C_NOPE:], pos)
    ckv = rms_norm(c_kv, mla_kv_norm_g)
    kv = (ckv @ w_ukv).reshape(b, s, C_HEADS, C_NOPE + C_DV)
    k_nope, v_c = kv[..., :C_NOPE], kv[..., C_NOPE:]
    k_pe = jnp.broadcast_to(rope(c_kr[:, :, None, :], pos), (b, s, C_HEADS, C_ROPE))
    qm = jnp.concatenate([q_nope, q_pe], -1).transpose(0, 2, 1, 3)
    km = jnp.concatenate([k_nope, k_pe], -1).transpose(0, 2, 1, 3)
    o_c = mla_attention(qm, km, v_c.transpose(0, 2, 1, 3))
    o_c = o_c.transpose(0, 2, 1, 3).reshape(b, s, C_WIDTH)

    return jnp.concatenate([o_a, o_b, o_c], -1) @ w_o


def conv_ffn(h, w_gate, w_up, conv_w, w_down):
    g = causal_dwconv3(h @ w_gate, conv_w)
    return (jax.nn.silu(g) * (h @ w_up)) @ w_down


def setup_inputs(seed: int = 0) -> dict:
    key = jax.random.key(seed)
    ks = jax.random.split(key, 24)
    n = jax.random.normal
    f32 = jnp.float32
    L = DEPTH
    x = n(ks[0], (BATCH, SEQ, D_MODEL), f32)
    offset = jax.random.randint(ks[1], (BATCH, 1), 0, 64, dtype=jnp.int32) * CHUNK
    positions = (offset + jnp.arange(SEQ, dtype=jnp.int32)[None, :]).astype(jnp.int32)
    return {
        "x": x,
        "positions": positions,
        "ln_in_g": 1.0 + 0.02 * n(ks[2], (D_MODEL,), f32),
        "ln_in_b": 0.02 * n(ks[3], (D_MODEL,), f32),
        "w_in": n(ks[4], (L, D_MODEL, P_IN), f32) * D_MODEL ** -0.5,
        "diff_lambda": 0.1 * n(ks[5], (L, 4, A_DHALF), f32),
        "diff_norm_g": 1.0 + 0.02 * n(ks[6], (L, A_DV), f32),
        "conv_w": n(ks[7], (L, CONV_W, B_WIDTH), f32) * CONV_W ** -0.5,
        "mla_q_norm_g": 1.0 + 0.02 * n(ks[8], (L, C_Q_RANK), f32),
        "mla_kv_norm_g": 1.0 + 0.02 * n(ks[9], (L, C_KV_RANK), f32),
        "w_uq": n(ks[10], (L, C_Q_RANK, C_HEADS * (C_NOPE + C_ROPE)), f32) * C_Q_RANK ** -0.5,
        "w_ukv": n(ks[11], (L, C_KV_RANK, C_HEADS * (C_NOPE + C_DV)), f32) * C_KV_RANK ** -0.5,
        "w_o": n(ks[12], (L, D_MIX, D_MODEL), f32) * (D_MIX ** -0.5 * BETA),
        "ln1_g": 1.0 + 0.02 * n(ks[13], (L, D_MODEL), f32),
        "ln1_b": 0.02 * n(ks[14], (L, D_MODEL), f32),
        "ffn_w_gate": n(ks[15], (L, D_MODEL, D_FF), f32) * D_MODEL ** -0.5,
        "ffn_w_up": n(ks[16], (L, D_MODEL, D_FF), f32) * D_MODEL ** -0.5,
        "ffn_conv_w": n(ks[17], (L, CONV_W, D_FF), f32) * CONV_W ** -0.5,
        "ffn_w_down": n(ks[18], (L, D_FF, D_MODEL), f32) * (D_FF ** -0.5 * BETA),
        "ln2_g": 1.0 + 0.02 * n(ks[19], (L, D_MODEL), f32),
        "ln2_b": 0.02 * n(ks[20], (L, D_MODEL), f32),
    }


def reference(x, positions, ln_in_g, ln_in_b, w_in, diff_lambda, diff_norm_g, conv_w,
              mla_q_norm_g, mla_kv_norm_g, w_uq, w_ukv, w_o, ln1_g, ln1_b,
              ffn_w_gate, ffn_w_up, ffn_conv_w, ffn_w_down, ln2_g, ln2_b):
    h = layer_norm(x, ln_in_g, ln_in_b)
    for l in range(DEPTH):
        lam_init = 0.8 - 0.6 * math.exp(-0.3 * l)
        m = mixer(h, positions, lam_init, w_in[l], diff_lambda[l], diff_norm_g[l], conv_w[l],
                  mla_q_norm_g[l], mla_kv_norm_g[l], w_uq[l], w_ukv[l], w_o[l])
        h = layer_norm(ALPHA * h + m, ln1_g[l], ln1_b[l])
        f = conv_ffn(h, ffn_w_gate[l], ffn_w_up[l], ffn_conv_w[l], ffn_w_down[l])
        h = layer_norm(ALPHA * h + f, ln2_g[l], ln2_b[l])
    return h
```

```python
import functools
import math

import jax
import jax.numpy as jnp
from jax import lax
from jax.experimental import pallas as pl
from jax.experimental.pallas import tpu as pltpu

F32 = jnp.float32
BF16 = jnp.bfloat16

D_MODEL = 2048
SEQ = 16384
DEPTH = 2
CHUNK = 64
N_HEADS = 6
HEAD_W = 128
A_DHALF = 64
A_WIDTH = N_HEADS * HEAD_W
B_WIDTH = 512
C_NOPE = 64
C_ROPE = 32
C_WIDTH = N_HEADS * HEAD_W
C_Q_RANK = 512
C_KV_RANK = 256
ROPE_THETA = 10000.0
D_FF = 5632
ALPHA = (2.0 * DEPTH) ** 0.25
LN_EPS = 1e-5
RMS_EPS = 1e-6
NEG = -1e30
LOG2E = math.log2(math.e)

LANES = 128
TS = 512
TQ = 256
TK = 512
TF = 512
HALO = 8
VMEM_LIMIT = 56 * 1024 * 1024


def _cparams(sem):
    return pltpu.CompilerParams(dimension_semantics=sem, vmem_limit_bytes=VMEM_LIMIT)


def _layer_norm(x, g, b):
    mu = jnp.mean(x, axis=-1, keepdims=True)
    xc = x - mu
    var = jnp.mean(xc * xc, axis=-1, keepdims=True)
    return xc * lax.rsqrt(var + LN_EPS) * g + b


def _rms_norm(x, g):
    return x * lax.rsqrt(jnp.mean(x * x, axis=-1, keepdims=True) + RMS_EPS) * g


def _dot(a, b):
    return jnp.dot(a, b, preferred_element_type=F32)


def _dot_nt(a, b):
    return lax.dot_general(a, b, (((1,), (1,)), ((), ())), preferred_element_type=F32)


def _ln_in_kernel(x_ref, g_ref, b_ref, h_ref, hb_ref):
    h = _layer_norm(x_ref[...], g_ref[...], b_ref[...])
    h_ref[...] = h
    hb_ref[...] = h.astype(BF16)


def _ln_in(x, g, b):
    s, d = x.shape
    row = pl.BlockSpec((TS, d), lambda i: (i, 0))
    vec = pl.BlockSpec((1, d), lambda i: (0, 0))
    return pl.pallas_call(
        _ln_in_kernel,
        out_shape=(jax.ShapeDtypeStruct((s, d), F32), jax.ShapeDtypeStruct((s, d), BF16)),
        grid=(s // TS,),
        in_specs=[row, vec, vec],
        out_specs=(row, row),
        compiler_params=_cparams(("parallel",)),
        name="ln_in",
    )(x, g.reshape(1, d), b.reshape(1, d))


def _rope_tab_kernel(pos_ref, invf_ref, ma_ref, mb_ref, cos_ref, sina_ref, sinb_ref):
    ang = pos_ref[...].astype(F32) * invf_ref[...]
    sn = jnp.sin(ang)
    cos_ref[...] = jnp.cos(ang)
    sina_ref[...] = sn * ma_ref[...]
    sinb_ref[...] = sn * mb_ref[...]


def _rope_tables(positions):
    s = positions.shape[0]
    half = C_ROPE // 2
    inv_freq = ROPE_THETA ** (-jnp.arange(0, C_ROPE, 2, dtype=F32) / C_ROPE)
    zeros = jnp.zeros((LANES,), F32)
    invf = zeros.at[C_NOPE:C_NOPE + half].set(inv_freq).at[C_NOPE + half:C_NOPE + C_ROPE].set(inv_freq)
    mask_a = zeros.at[C_NOPE + half:C_NOPE + C_ROPE].set(1.0)
    mask_b = zeros.at[C_NOPE:C_NOPE + half].set(-1.0)
    tab = jax.ShapeDtypeStruct((s, LANES), F32)
    row = pl.BlockSpec((TS, LANES), lambda i: (i, 0))
    vec = pl.BlockSpec((1, LANES), lambda i: (0, 0))
    return pl.pallas_call(
        _rope_tab_kernel,
        out_shape=(tab, tab, tab),
        grid=(s // TS,),
        in_specs=[pl.BlockSpec((TS, 1), lambda i: (i, 0)), vec, vec, vec],
        out_specs=(row, row, row),
        compiler_params=_cparams(("parallel",)),
        name="rope_tables",
    )(positions.reshape(s, 1), invf.reshape(1, LANES), mask_a.reshape(1, LANES), mask_b.reshape(1, LANES))


def _rope(x, cos, sina, sinb):
    half = C_ROPE // 2
    return x * cos + pltpu.roll(x, half, 1) * sina + pltpu.roll(x, LANES - half, 1) * sinb


def _proj_qk_kernel(x_ref, w_ref, s_ref, o_ref):
    o_ref[...] = (_dot(x_ref[...], w_ref[...]) * s_ref[...]).astype(BF16)


def _proj_qk(hb, w, col_scale):
    s, d = hb.shape
    n = w.shape[1]
    tn = A_WIDTH
    return pl.pallas_call(
        _proj_qk_kernel,
        out_shape=jax.ShapeDtypeStruct((s, n), BF16),
        grid=(n // tn, s // TS),
        in_specs=[pl.BlockSpec((TS, d), lambda j, i: (i, 0)),
                  pl.BlockSpec((d, tn), lambda j, i: (0, j)),
                  pl.BlockSpec((1, tn), lambda j, i: (0, j))],
        out_specs=pl.BlockSpec((TS, tn), lambda j, i: (i, j)),
        compiler_params=_cparams(("parallel", "parallel")),
        name="proj_qk",
    )(hb, w, col_scale)


def _proj_vt_kernel(x_ref, w_ref, o_ref):
    o_ref[...] = _dot_nt(w_ref[...], x_ref[...]).astype(BF16)


def _proj_vt(hb, w_t):
    s, d = hb.shape
    n = w_t.shape[0]
    return pl.pallas_call(
        _proj_vt_kernel,
        out_shape=jax.ShapeDtypeStruct((s // TK, n, TK), BF16),
        grid=(s // TK,),
        in_specs=[pl.BlockSpec((TK, d), lambda i: (i, 0)),
                  pl.BlockSpec((n, d), lambda i: (0, 0))],
        out_specs=pl.BlockSpec((None, n, TK), lambda i: (i, 0, 0)),
        compiler_params=_cparams(("parallel",)),
        name="proj_vt",
    )(hb, w_t)


def _causal_conv3(buf_ref, cw_ref, rows):
    return (cw_ref[0:1, :] * buf_ref[pl.ds(HALO - 2, rows), :]
            + cw_ref[1:2, :] * buf_ref[pl.ds(HALO - 1, rows), :]
            + cw_ref[2:3, :] * buf_ref[pl.ds(HALO, rows), :])


def _mix_b_kernel(x_ref, w_ref, cw_ref, o_ref, ubuf):
    i = pl.program_id(0)

    @pl.when(i == 0)
    def _():
        ubuf[pl.ds(0, HALO), :] = jnp.zeros((HALO, B_WIDTH), F32)

    p = _dot(x_ref[...], w_ref[...])
    ubuf[pl.ds(HALO, TS), :] = p[:, 2 * B_WIDTH:] * p[:, :B_WIDTH]
    y = _causal_conv3(ubuf, cw_ref, TS)
    o_ref[...] = (p[:, B_WIDTH:2 * B_WIDTH] * y).astype(BF16)
    ubuf[pl.ds(0, HALO), :] = ubuf[pl.ds(TS, HALO), :]


def _mix_b(hb, w, conv_w):
    s, d = hb.shape
    return pl.pallas_call(
        _mix_b_kernel,
        out_shape=jax.ShapeDtypeStruct((s, B_WIDTH), BF16),
        grid=(s // TS,),
        in_specs=[pl.BlockSpec((TS, d), lambda i: (i, 0)),
                  pl.BlockSpec((d, 3 * B_WIDTH), lambda i: (0, 0)),
                  pl.BlockSpec((3, B_WIDTH), lambda i: (0, 0))],
        out_specs=pl.BlockSpec((TS, B_WIDTH), lambda i: (i, 0)),
        scratch_shapes=[pltpu.VMEM((HALO + TS, B_WIDTH), F32)],
        compiler_params=_cparams(("arbitrary",)),
        name="mix_b",
    )(hb, w, conv_w)


def _proj_c_kernel(x_ref, wc_ref, gq_ref, gkv_ref, wuq_ref, wuk_ref, wuvt_ref,
                   cos_ref, sina_ref, sinb_ref, q_ref, k_ref, vt_ref, *, q_scale):
    p = _dot(x_ref[...], wc_ref[...])
    cos, sina, sinb = cos_ref[...], sina_ref[...], sinb_ref[...]
    cq = _rms_norm(p[:, :C_Q_RANK], gq_ref[...]).astype(BF16)
    ckv = _rms_norm(p[:, C_Q_RANK:C_Q_RANK + C_KV_RANK], gkv_ref[...]).astype(BF16)
    k_pe = _rope(p[:, C_Q_RANK + C_KV_RANK:], cos, sina, sinb)
    qc = _dot(cq, wuq_ref[...])
    kc = _dot(ckv, wuk_ref[...])
    for h in range(N_HEADS):
        sl = slice(h * HEAD_W, (h + 1) * HEAD_W)
        q_ref[:, sl] = (_rope(qc[:, sl], cos, sina, sinb) * q_scale).astype(BF16)
        k_ref[:, sl] = (kc[:, sl] + k_pe).astype(BF16)
    vt_ref[...] = _dot_nt(wuvt_ref[...], ckv).astype(BF16)


def _proj_c(hb, wc, gq, gkv, wuq, wuk, wuvt, cos, sina, sinb, q_scale):
    s, d = hb.shape
    nc = wc.shape[1]
    row = pl.BlockSpec((TK, C_WIDTH), lambda i: (i, 0))
    tab = pl.BlockSpec((TK, LANES), lambda i: (i, 0))

    def full(a):
        return pl.BlockSpec(a.shape, lambda i: (0,) * a.ndim)

    return pl.pallas_call(
        functools.partial(_proj_c_kernel, q_scale=q_scale),
        out_shape=(jax.ShapeDtypeStruct((s, C_WIDTH), BF16),
                   jax.ShapeDtypeStruct((s, C_WIDTH), BF16),
                   jax.ShapeDtypeStruct((s // TK, C_WIDTH, TK), BF16)),
        grid=(s // TK,),
        in_specs=[pl.BlockSpec((TK, d), lambda i: (i, 0)), full(wc), full(gq), full(gkv),
                  full(wuq), full(wuk), full(wuvt), tab, tab, tab],
        out_specs=(row, row, pl.BlockSpec((None, C_WIDTH, TK), lambda i: (i, 0, 0))),
        compiler_params=_cparams(("parallel",)),
        name="proj_c",
    )(hb, wc, gq, gkv, wuq, wuk, wuvt, cos, sina, sinb)


def _chunk_mask(i, j):
    kidx = j * TK + lax.broadcasted_iota(jnp.int32, (TK, TQ), 0)
    qidx = i * TQ + lax.broadcasted_iota(jnp.int32, (TK, TQ), 1)
    return (kidx // CHUNK) <= (qidx // CHUNK)


def _softmax_step(s_t, vt, m, l, acc_ref):
    m_new = jnp.maximum(m, jnp.max(s_t, axis=0, keepdims=True))
    alpha = jnp.exp2(m - m_new)
    p = jnp.exp2(s_t - m_new)
    l_new = alpha * l + jnp.sum(p, axis=0, keepdims=True)
    acc_ref[...] = acc_ref[...] * alpha + _dot(vt, p.astype(BF16))
    return m_new, l_new


def _attn_a_kernel(slope_ref, q_ref, k_ref, vt_ref, pq_ref, pk_ref, dl_ref, g_ref, o_ref,
                   acc1, acc2, *, lam_init):
    h = pl.program_id(0)
    i = pl.program_id(1)
    q = q_ref[...]
    lane = lax.broadcasted_iota(jnp.int32, (TQ, HEAD_W), 1)
    zero = jnp.zeros_like(q)
    q1 = jnp.where(lane < A_DHALF, q, zero)
    q2 = jnp.where(lane >= A_DHALF, q, zero)
    slope = slope_ref[h]
    aq = pq_ref[...] * slope
    acc1[...] = jnp.zeros_like(acc1)
    acc2[...] = jnp.zeros_like(acc2)

    def step(j, carry, masked):
        m1, l1, m2, l2 = carry
        off = pl.multiple_of(j * TK, TK)
        k = k_ref[pl.ds(off, TK), :]
        vt = vt_ref[j]
        bk = pk_ref[pl.ds(off, TK), :] * slope
        bias = -jnp.abs(aq - jnp.concatenate([bk] * (TQ // LANES), axis=1))
        if masked:
            bias = jnp.where(_chunk_mask(i, j), bias, NEG)
        m1, l1 = _softmax_step(_dot_nt(k, q1) + bias, vt, m1, l1, acc1)
        m2, l2 = _softmax_step(_dot_nt(k, q2) + bias, vt, m2, l2, acc2)
        return m1, l1, m2, l2

    n_full = (i * TQ) // TK
    init = (jnp.full((1, TQ), NEG, F32), jnp.zeros((1, TQ), F32)) * 2
    carry = lax.fori_loop(0, n_full, functools.partial(step, masked=False), init)
    m1, l1, m2, l2 = step(n_full, carry, True)

    dl = dl_ref[...]
    lam = (jnp.exp(jnp.sum(dl[0:1] * dl[1:2], axis=1, keepdims=True))
           - jnp.exp(jnp.sum(dl[2:3] * dl[3:4], axis=1, keepdims=True)) + lam_init)
    o = acc1[...] * (1.0 / l1) - lam * (acc2[...] * (1.0 / l2))
    r = lax.rsqrt(jnp.mean(o * o, axis=0, keepdims=True) + RMS_EPS)
    g = jnp.concatenate([g_ref[...]] * (TQ // LANES), axis=1)
    o_ref[...] = (o * r * g * (1.0 - lam_init)).T.astype(BF16)


def _attn_a(slopes, qk, vt, pq, pk, dl, g_b, lam_init):
    s = qk.shape[0]
    return pl.pallas_call(
        functools.partial(_attn_a_kernel, lam_init=lam_init),
        out_shape=jax.ShapeDtypeStruct((s, A_WIDTH), BF16),
        grid=(N_HEADS, s // TQ),
        in_specs=[pl.BlockSpec(memory_space=pltpu.SMEM),
                  pl.BlockSpec((TQ, HEAD_W), lambda h, i: (i, h)),
                  pl.BlockSpec((s, HEAD_W), lambda h, i: (0, N_HEADS + h)),
                  pl.BlockSpec((s // TK, HEAD_W, TK), lambda h, i: (0, h, 0)),
                  pl.BlockSpec((1, TQ), lambda h, i: (0, i)),
                  pl.BlockSpec((s, LANES), lambda h, i: (0, 0)),
                  pl.BlockSpec((4, A_DHALF), lambda h, i: (0, 0)),
                  pl.BlockSpec((HEAD_W, LANES), lambda h, i: (0, 0))],
        out_specs=pl.BlockSpec((TQ, HEAD_W), lambda h, i: (i, h)),
        scratch_shapes=[pltpu.VMEM((HEAD_W, TQ), F32), pltpu.VMEM((HEAD_W, TQ), F32)],
        compiler_params=_cparams(("parallel", "parallel")),
        name="attn_a",
    )(slopes, qk, qk, vt, pq, pk, dl, g_b)


def _attn_c_kernel(q_ref, k_ref, vt_ref, o_ref, acc):
    i = pl.program_id(1)
    q = q_ref[...]
    acc[...] = jnp.zeros_like(acc)

    def step(j, carry, masked):
        m, l = carry
        off = pl.multiple_of(j * TK, TK)
        s_t = _dot_nt(k_ref[pl.ds(off, TK), :], q)
        if masked:
            s_t = jnp.where(_chunk_mask(i, j), s_t, NEG)
        return _softmax_step(s_t, vt_ref[j], m, l, acc)

    n_full = (i * TQ) // TK
    init = (jnp.full((1, TQ), NEG, F32), jnp.zeros((1, TQ), F32))
    carry = lax.fori_loop(0, n_full, functools.partial(step, masked=False), init)
    _, l = step(n_full, carry, True)
    o_ref[...] = (acc[...] * (1.0 / l)).T.astype(BF16)


def _attn_c(q, k, vt):
    s = q.shape[0]
    return pl.pallas_call(
        _attn_c_kernel,
        out_shape=jax.ShapeDtypeStruct((s, C_WIDTH), BF16),
        grid=(N_HEADS, s // TQ),
        in_specs=[pl.BlockSpec((TQ, HEAD_W), lambda h, i: (i, h)),
                  pl.BlockSpec((s, HEAD_W), lambda h, i: (0, h)),
                  pl.BlockSpec((s // TK, HEAD_W, TK), lambda h, i: (0, h, 0))],
        out_specs=pl.BlockSpec((TQ, HEAD_W), lambda h, i: (i, h)),
        scratch_shapes=[pltpu.VMEM((HEAD_W, TQ), F32)],
        compiler_params=_cparams(("parallel", "parallel")),
        name="attn_c",
    )(q, k, vt)


def _out_ln_kernel(oa_ref, ob_ref, oc_ref, wa_ref, wb_ref, wc_ref, h_ref, g_ref, b_ref,
                   y_ref, yb_ref):
    m = _dot(oa_ref[...], wa_ref[...]) + _dot(ob_ref[...], wb_ref[...]) + _dot(oc_ref[...], wc_ref[...])
    y = _layer_norm(ALPHA * h_ref[...] + m, g_ref[...], b_ref[...])
    y_ref[...] = y
    yb_ref[...] = y.astype(BF16)


def _out_ln(oa, ob, oc, wa, wb, wc, h, g, b):
    s, d = h.shape
    row = pl.BlockSpec((TS, d), lambda i: (i, 0))
    vec = pl.BlockSpec((1, d), lambda i: (0, 0))

    def rows(a):
        return pl.BlockSpec((TS, a.shape[1]), lambda i: (i, 0))

    def full(a):
        return pl.BlockSpec(a.shape, lambda i: (0, 0))

    return pl.pallas_call(
        _out_ln_kernel,
        out_shape=(jax.ShapeDtypeStruct((s, d), F32), jax.ShapeDtypeStruct((s, d), BF16)),
        grid=(s // TS,),
        in_specs=[rows(oa), rows(ob), rows(oc), full(wa), full(wb), full(wc), row, vec, vec],
        out_specs=(row, row),
        compiler_params=_cparams(("parallel",)),
        name="out_ln",
    )(oa, ob, oc, wa, wb, wc, h, g.reshape(1, d), b.reshape(1, d))


def _ffn_kernel(xb_ref, wg_ref, wu_ref, cw_ref, wd_ref, h_ref, g_ref, b_ref, y_ref, yb_ref,
                gbuf, carry, acc):
    i = pl.program_id(0)
    j = pl.program_id(1)
    x = xb_ref[...]

    @pl.when(i == 0)
    def _():
        carry[j] = jnp.zeros((HALO, TF), F32)

    gbuf[pl.ds(0, HALO), :] = carry[j]
    gbuf[pl.ds(HALO, TS), :] = _dot(x, wg_ref[...])
    carry[j] = gbuf[pl.ds(TS, HALO), :]
    gate = _causal_conv3(gbuf, cw_ref, TS)
    act = (gate * (1.0 / (1.0 + jnp.exp(-gate))) * _dot(x, wu_ref[...])).astype(BF16)
    part = _dot(act, wd_ref[...])

    @pl.when(j == 0)
    def _():
        acc[...] = part

    @pl.when(j > 0)
    def _():
        acc[...] += part

    @pl.when(j == pl.num_programs(1) - 1)
    def _():
        y = _layer_norm(ALPHA * h_ref[...] + acc[...], g_ref[...], b_ref[...])
        y_ref[...] = y
        yb_ref[...] = y.astype(BF16)


def _ffn(xb, wg, wu, conv_w, wd, h, g, b):
    s, d = h.shape
    nf = D_FF // TF
    row = pl.BlockSpec((TS, d), lambda i, j: (i, 0))
    vec = pl.BlockSpec((1, d), lambda i, j: (0, 0))
    return pl.pallas_call(
        _ffn_kernel,
        out_shape=(jax.ShapeDtypeStruct((s, d), F32), jax.ShapeDtypeStruct((s, d), BF16)),
        grid=(s // TS, nf),
        in_specs=[row,
                  pl.BlockSpec((d, TF), lambda i, j: (0, j)),
                  pl.BlockSpec((d, TF), lambda i, j: (0, j)),
                  pl.BlockSpec((3, TF), lambda i, j: (0, j)),
                  pl.BlockSpec((TF, d), lambda i, j: (j, 0)),
                  row, vec, vec],
        out_specs=(row, row),
        scratch_shapes=[pltpu.VMEM((HALO + TS, TF), F32),
                        pltpu.VMEM((nf, HALO, TF), F32),
                        pltpu.VMEM((TS, d), F32)],
        compiler_params=_cparams(("arbitrary", "arbitrary")),
        name="ffn",
    )(xb, wg, wu, conv_w, wd, h, g.reshape(1, d), b.reshape(1, d))


def _pad_heads(w, width):
    kdim = w.shape[0]
    w = w.reshape(kdim, N_HEADS, width)
    return jnp.pad(w, ((0, 0), (0, 0), (0, HEAD_W - width))).reshape(kdim, N_HEADS * HEAD_W)


def _prep_layer(w_in, w_uq, w_ukv, w_o):
    aq, ak, av = 0, A_WIDTH, 2 * A_WIDTH
    bh = 3 * A_WIDTH
    cq = bh + 3 * B_WIDTH
    ckv = cq + C_Q_RANK
    ckr = ckv + C_KV_RANK
    kr_tile = jnp.pad(w_in[:, ckr:], ((0, 0), (C_NOPE, HEAD_W - C_NOPE - C_ROPE)))
    ukv = w_ukv.reshape(C_KV_RANK, N_HEADS, C_NOPE + HEAD_W)
    return dict(
        w_qk=w_in[:, aq:av].astype(BF16),
        w_vt=w_in[:, av:bh].T.astype(BF16),
        w_b=w_in[:, bh:cq].astype(BF16),
        w_c=jnp.concatenate([w_in[:, cq:ckr], kr_tile], axis=1).astype(BF16),
        w_uq=_pad_heads(w_uq, C_NOPE + C_ROPE).astype(BF16),
        w_uk=_pad_heads(ukv[:, :, :C_NOPE].reshape(C_KV_RANK, N_HEADS * C_NOPE), C_NOPE).astype(BF16),
        w_uvt=ukv[:, :, C_NOPE:].reshape(C_KV_RANK, C_WIDTH).T.astype(BF16),
        w_oa=w_o[:A_WIDTH].astype(BF16),
        w_ob=w_o[A_WIDTH:A_WIDTH + B_WIDTH].astype(BF16),
        w_oc=w_o[A_WIDTH + B_WIDTH:].astype(BF16),
    )


def kernel(x, positions, ln_in_g, ln_in_b, w_in, diff_lambda, diff_norm_g, conv_w, mla_q_norm_g, mla_kv_norm_g, w_uq, w_ukv, w_o, ln1_g, ln1_b, ffn_w_gate, ffn_w_up, ffn_conv_w, ffn_w_down, ln2_g, ln2_b):
    batch, s, d = x.shape
    assert (batch, s, d) == (1, SEQ, D_MODEL)
    pos = positions.reshape(s)
    pos_f = pos.astype(F32)
    pq = pos_f.reshape(1, s)
    pk = jnp.broadcast_to(pos_f.reshape(s, 1), (s, LANES))
    heads = jnp.arange(N_HEADS, dtype=F32)
    slopes = (2.0 ** (-8.0 * (heads + 1.0) / N_HEADS)) * LOG2E
    qk_scale = jnp.concatenate([jnp.full((1, A_WIDTH), A_DHALF ** -0.5 * LOG2E, F32),
                                jnp.ones((1, A_WIDTH), F32)], axis=1)
    c_scale = (C_NOPE + C_ROPE) ** -0.5 * LOG2E

    cos, sina, sinb = _rope_tables(pos)
    h, hb = _ln_in(x.reshape(s, d), ln_in_g, ln_in_b)
    for l in range(DEPTH):
        lam_init = 0.8 - 0.6 * math.exp(-0.3 * l)
        w = _prep_layer(w_in[l], w_uq[l], w_ukv[l], w_o[l])
        qk = _proj_qk(hb, w["w_qk"], qk_scale)
        vt_a = _proj_vt(hb, w["w_vt"])
        g_b = jnp.broadcast_to(diff_norm_g[l].reshape(HEAD_W, 1), (HEAD_W, LANES))
        o_a = _attn_a(slopes, qk, vt_a, pq, pk, diff_lambda[l], g_b, lam_init)
        o_b = _mix_b(hb, w["w_b"], conv_w[l])
        q_c, k_c, vt_c = _proj_c(hb, w["w_c"], mla_q_norm_g[l].reshape(1, -1), mla_kv_norm_g[l].reshape(1, -1),
                                 w["w_uq"], w["w_uk"], w["w_uvt"], cos, sina, sinb, c_scale)
        o_c = _attn_c(q_c, k_c, vt_c)
        h, hb = _out_ln(o_a, o_b, o_c, w["w_oa"], w["w_ob"], w["w_oc"], h, ln1_g[l], ln1_b[l])
        h, hb = _ffn(hb, ffn_w_gate[l].astype(BF16), ffn_w_up[l].astype(BF16), ffn_conv_w[l],
                     ffn_w_down[l].astype(BF16), h, ln2_g[l], ln2_b[l])
    return h.reshape(batch, s, d)
```

```python
import functools
import math

import jax
import jax.numpy as jnp
from jax import lax
from jax.experimental import pallas as pl
from jax.experimental.pallas import tpu as pltpu

F32 = jnp.float32
BF16 = jnp.bfloat16

D_MODEL = 2048
SEQ = 16384
DEPTH = 2
CHUNK = 64
N_HEADS = 6
HEAD_W = 128
A_DHALF = 64
A_WIDTH = N_HEADS * HEAD_W
B_WIDTH = 512
C_NOPE = 64
C_ROPE = 32
C_WIDTH = N_HEADS * HEAD_W
C_Q_RANK = 512
C_KV_RANK = 256
ROPE_THETA = 10000.0
D_FF = 5632
ALPHA = (2.0 * DEPTH) ** 0.25
LN_EPS = 1e-5
RMS_EPS = 1e-6
NEG = -1e30
LOG2E = math.log2(math.e)

LANES = 128
TS = 512
TQ_A = 256
TQ_C = 512
TK = 512
TF = 512
HALO = 8
VMEM_LIMIT = 56 * 1024 * 1024


def _cparams(sem):
    return pltpu.CompilerParams(dimension_semantics=sem, vmem_limit_bytes=VMEM_LIMIT)


def _layer_norm(x, g, b):
    mu = jnp.mean(x, axis=-1, keepdims=True)
    xc = x - mu
    var = jnp.mean(xc * xc, axis=-1, keepdims=True)
    return xc * lax.rsqrt(var + LN_EPS) * g + b


def _rms_norm(x, g):
    return x * lax.rsqrt(jnp.mean(x * x, axis=-1, keepdims=True) + RMS_EPS) * g


def _dot(a, b):
    return jnp.dot(a, b, preferred_element_type=F32)


def _dot_nt(a, b):
    return lax.dot_general(a, b, (((1,), (1,)), ((), ())), preferred_element_type=F32)


def _ln_in_kernel(x_ref, g_ref, b_ref, h_ref, hb_ref):
    h = _layer_norm(x_ref[...], g_ref[...], b_ref[...])
    h_ref[...] = h
    hb_ref[...] = h.astype(BF16)


def _ln_in(x, g, b):
    s, d = x.shape
    row = pl.BlockSpec((TS, d), lambda i: (i, 0))
    vec = pl.BlockSpec((1, d), lambda i: (0, 0))
    return pl.pallas_call(
        _ln_in_kernel,
        out_shape=(jax.ShapeDtypeStruct((s, d), F32), jax.ShapeDtypeStruct((s, d), BF16)),
        grid=(s // TS,),
        in_specs=[row, vec, vec],
        out_specs=(row, row),
        compiler_params=_cparams(("parallel",)),
        name="ln_in",
    )(x, g.reshape(1, d), b.reshape(1, d))


def _rope_tab_kernel(pos_ref, invf_ref, ma_ref, mb_ref, cos_ref, sina_ref, sinb_ref):
    ang = pos_ref[...].astype(F32) * invf_ref[...]
    sn = jnp.sin(ang)
    cos_ref[...] = jnp.cos(ang)
    sina_ref[...] = sn * ma_ref[...]
    sinb_ref[...] = sn * mb_ref[...]


def _rope_tables(positions):
    s = positions.shape[0]
    half = C_ROPE // 2
    inv_freq = ROPE_THETA ** (-jnp.arange(0, C_ROPE, 2, dtype=F32) / C_ROPE)
    zeros = jnp.zeros((LANES,), F32)
    invf = zeros.at[C_NOPE:C_NOPE + half].set(inv_freq).at[C_NOPE + half:C_NOPE + C_ROPE].set(inv_freq)
    mask_a = zeros.at[C_NOPE + half:C_NOPE + C_ROPE].set(1.0)
    mask_b = zeros.at[C_NOPE:C_NOPE + half].set(-1.0)
    tab = jax.ShapeDtypeStruct((s, LANES), F32)
    row = pl.BlockSpec((TS, LANES), lambda i: (i, 0))
    vec = pl.BlockSpec((1, LANES), lambda i: (0, 0))
    return pl.pallas_call(
        _rope_tab_kernel,
        out_shape=(tab, tab, tab),
        grid=(s // TS,),
        in_specs=[pl.BlockSpec((TS, 1), lambda i: (i, 0)), vec, vec, vec],
        out_specs=(row, row, row),
        compiler_params=_cparams(("parallel",)),
        name="rope_tables",
    )(positions.reshape(s, 1), invf.reshape(1, LANES), mask_a.reshape(1, LANES), mask_b.reshape(1, LANES))


def _rope(x, cos, sina, sinb):
    half = C_ROPE // 2
    return x * cos + pltpu.roll(x, half, 1) * sina + pltpu.roll(x, LANES - half, 1) * sinb


def _proj_qk_kernel(x_ref, w_ref, s_ref, o_ref):
    o_ref[...] = (_dot(x_ref[...], w_ref[...]) * s_ref[...]).astype(BF16)


def _proj_qk(hb, w, col_scale):
    s, d = hb.shape
    n = w.shape[1]
    tn = A_WIDTH
    return pl.pallas_call(
        _proj_qk_kernel,
        out_shape=jax.ShapeDtypeStruct((s, n), BF16),
        grid=(n // tn, s // TS),
        in_specs=[pl.BlockSpec((TS, d), lambda j, i: (i, 0)),
                  pl.BlockSpec((d, tn), lambda j, i: (0, j)),
                  pl.BlockSpec((1, tn), lambda j, i: (0, j))],
        out_specs=pl.BlockSpec((TS, tn), lambda j, i: (i, j)),
        compiler_params=_cparams(("parallel", "parallel")),
        name="proj_qk",
    )(hb, w, col_scale)


def _proj_vt_kernel(x_ref, w_ref, o_ref):
    o_ref[...] = _dot_nt(w_ref[...], x_ref[...]).astype(BF16)


def _proj_vt(hb, w_t):
    s, d = hb.shape
    n = w_t.shape[0]
    return pl.pallas_call(
        _proj_vt_kernel,
        out_shape=jax.ShapeDtypeStruct((s // TK, n, TK), BF16),
        grid=(s // TK,),
        in_specs=[pl.BlockSpec((TK, d), lambda i: (i, 0)),
                  pl.BlockSpec((n, d), lambda i: (0, 0))],
        out_specs=pl.BlockSpec((None, n, TK), lambda i: (i, 0, 0)),
        compiler_params=_cparams(("parallel",)),
        name="proj_vt",
    )(hb, w_t)


def _causal_conv3(buf_ref, cw_ref, rows):
    return (cw_ref[0:1, :] * buf_ref[pl.ds(HALO - 2, rows), :]
            + cw_ref[1:2, :] * buf_ref[pl.ds(HALO - 1, rows), :]
            + cw_ref[2:3, :] * buf_ref[pl.ds(HALO, rows), :])


def _mix_b_kernel(x_ref, w_ref, cw_ref, o_ref, ubuf):
    i = pl.program_id(0)

    @pl.when(i == 0)
    def _():
        ubuf[pl.ds(0, HALO), :] = jnp.zeros((HALO, B_WIDTH), F32)

    p = _dot(x_ref[...], w_ref[...])
    ubuf[pl.ds(HALO, TS), :] = p[:, 2 * B_WIDTH:] * p[:, :B_WIDTH]
    y = _causal_conv3(ubuf, cw_ref, TS)
    o_ref[...] = (p[:, B_WIDTH:2 * B_WIDTH] * y).astype(BF16)
    ubuf[pl.ds(0, HALO), :] = ubuf[pl.ds(TS, HALO), :]


def _mix_b(hb, w, conv_w):
    s, d = hb.shape
    return pl.pallas_call(
        _mix_b_kernel,
        out_shape=jax.ShapeDtypeStruct((s, B_WIDTH), BF16),
        grid=(s // TS,),
        in_specs=[pl.BlockSpec((TS, d), lambda i: (i, 0)),
                  pl.BlockSpec((d, 3 * B_WIDTH), lambda i: (0, 0)),
                  pl.BlockSpec((3, B_WIDTH), lambda i: (0, 0))],
        out_specs=pl.BlockSpec((TS, B_WIDTH), lambda i: (i, 0)),
        scratch_shapes=[pltpu.VMEM((HALO + TS, B_WIDTH), F32)],
        compiler_params=_cparams(("arbitrary",)),
        name="mix_b",
    )(hb, w, conv_w)


def _proj_c_kernel(x_ref, wc_ref, gq_ref, gkv_ref, wuq_ref, wuk_ref, wuvt_ref,
                   cos_ref, sina_ref, sinb_ref, q_ref, k_ref, vt_ref, *, q_scale):
    p = _dot(x_ref[...], wc_ref[...])
    cos, sina, sinb = cos_ref[...], sina_ref[...], sinb_ref[...]
    cq = _rms_norm(p[:, :C_Q_RANK], gq_ref[...]).astype(BF16)
    ckv = _rms_norm(p[:, C_Q_RANK:C_Q_RANK + C_KV_RANK], gkv_ref[...]).astype(BF16)
    k_pe = _rope(p[:, C_Q_RANK + C_KV_RANK:], cos, sina, sinb)
    qc = _dot(cq, wuq_ref[...])
    kc = _dot(ckv, wuk_ref[...])
    for h in range(N_HEADS):
        sl = slice(h * HEAD_W, (h + 1) * HEAD_W)
        q_ref[:, sl] = (_rope(qc[:, sl], cos, sina, sinb) * q_scale).astype(BF16)
        k_ref[:, sl] = (kc[:, sl] + k_pe).astype(BF16)
    vt_ref[...] = _dot_nt(wuvt_ref[...], ckv).astype(BF16)


def _proj_c(hb, wc, gq, gkv, wuq, wuk, wuvt, cos, sina, sinb, q_scale):
    s, d = hb.shape
    nc = wc.shape[1]
    row = pl.BlockSpec((TK, C_WIDTH), lambda i: (i, 0))
    tab = pl.BlockSpec((TK, LANES), lambda i: (i, 0))

    def full(a):
        return pl.BlockSpec(a.shape, lambda i: (0,) * a.ndim)

    return pl.pallas_call(
        functools.partial(_proj_c_kernel, q_scale=q_scale),
        out_shape=(jax.ShapeDtypeStruct((s, C_WIDTH), BF16),
                   jax.ShapeDtypeStruct((s, C_WIDTH), BF16),
                   jax.ShapeDtypeStruct((s // TK, C_WIDTH, TK), BF16)),
        grid=(s // TK,),
        in_specs=[pl.BlockSpec((TK, d), lambda i: (i, 0)), full(wc), full(gq), full(gkv),
                  full(wuq), full(wuk), full(wuvt), tab, tab, tab],
        out_specs=(row, row, pl.BlockSpec((None, C_WIDTH, TK), lambda i: (i, 0, 0))),
        compiler_params=_cparams(("parallel",)),
        name="proj_c",
    )(hb, wc, gq, gkv, wuq, wuk, wuvt, cos, sina, sinb)


ST_M, ST_L, ST_ALPHA, ST_BM, ST_ROWS = 0, 1, 2, 3, 4


def _stat(st_ref, x, row):
    r = x * ST_ROWS + row
    return st_ref.at[r:r + 1, :]


def _online_softmax(i, tq, k_ref, vt_ref, qs, s_refs, p_refs, acc_refs, st_ref, bias_fn):
    n_maps = len(qs)
    n_full = (i * tq) // TK

    def scores(b, masked):
        off = pl.multiple_of(b * TK, TK)
        k = k_ref[pl.ds(off, TK), :]
        bias = None if bias_fn is None else bias_fn(off)
        if masked:
            kidx = b * TK + lax.broadcasted_iota(jnp.int32, (TK, tq), 0)
            qidx = i * tq + lax.broadcasted_iota(jnp.int32, (TK, tq), 1)
            mask = (kidx // CHUNK) <= (qidx // CHUNK)
            if bias is not None:
                bias = jnp.where(mask, bias, NEG)
        for x in range(n_maps):
            s = _dot_nt(k, qs[x])
            if bias is not None:
                s = s + bias
            elif masked:
                s = jnp.where(mask, s, NEG)
            s_refs[x][...] = s
            _stat(st_ref, x, ST_BM)[...] = jnp.max(s, axis=0, keepdims=True)

    def numerators(x):
        m_old = _stat(st_ref, x, ST_M)[...]
        m_new = jnp.maximum(m_old, _stat(st_ref, x, ST_BM)[...])
        alpha = jnp.exp2(m_old - m_new)
        p = jnp.exp2(s_refs[x][...] - m_new)
        _stat(st_ref, x, ST_L)[...] = alpha * _stat(st_ref, x, ST_L)[...] + jnp.sum(p, axis=0, keepdims=True)
        _stat(st_ref, x, ST_M)[...] = m_new
        _stat(st_ref, x, ST_ALPHA)[...] = alpha
        p_refs[x][...] = p.astype(BF16)

    def values(b):
        vt = vt_ref[jnp.maximum(b, 0)]
        for x in range(n_maps):
            acc_refs[x][...] = acc_refs[x][...] * _stat(st_ref, x, ST_ALPHA)[...] + _dot(vt, p_refs[x][...])

    def iteration(b, next_scores):
        values(b - 1)
        for x in range(n_maps):
            numerators(x)
        if next_scores is not None:
            scores(b + 1, masked=next_scores == "masked")

    for x in range(n_maps):
        p_refs[x][...] = jnp.zeros_like(p_refs[x])
        acc_refs[x][...] = jnp.zeros_like(acc_refs[x])
        _stat(st_ref, x, ST_M)[...] = jnp.full((1, tq), NEG, F32)
        _stat(st_ref, x, ST_L)[...] = jnp.zeros((1, tq), F32)
        _stat(st_ref, x, ST_ALPHA)[...] = jnp.ones((1, tq), F32)
    scores(0, masked=True)

    def body(b, carry):
        iteration(b, "plain")
        return carry

    lax.fori_loop(0, n_full - 1, body, 0)

    @pl.when(n_full > 0)
    def _():
        iteration(n_full - 1, "masked")

    iteration(n_full, None)
    values(n_full)


def _attn_a_kernel(slope_ref, q_ref, k_ref, vt_ref, pq_ref, pk_ref, dl_ref, g_ref, o_ref,
                   s1, s2, p1, p2, acc1, acc2, st, *, lam_init):
    h = pl.program_id(0)
    i = pl.program_id(1)
    q = q_ref[...]
    lane = lax.broadcasted_iota(jnp.int32, (TQ_A, HEAD_W), 1)
    zero = jnp.zeros_like(q)
    q1 = jnp.where(lane < A_DHALF, q, zero)
    q2 = jnp.where(lane >= A_DHALF, q, zero)
    slope = slope_ref[h]
    aq = pq_ref[...] * slope

    def bias_fn(off):
        bk = pk_ref[pl.ds(off, TK), :] * slope
        d = aq - jnp.concatenate([bk] * (TQ_A // LANES), axis=1)
        return pltpu.bitcast(pltpu.bitcast(d, jnp.uint32) | jnp.uint32(0x80000000), F32)

    _online_softmax(i, TQ_A, k_ref, vt_ref, (q1, q2), (s1, s2), (p1, p2), (acc1, acc2), st, bias_fn)

    dl = dl_ref[...]
    lam = (jnp.exp(jnp.sum(dl[0:1] * dl[1:2], axis=1, keepdims=True))
           - jnp.exp(jnp.sum(dl[2:3] * dl[3:4], axis=1, keepdims=True)) + lam_init)
    o = (acc1[...] * (1.0 / _stat(st, 0, ST_L)[...])
         - lam * (acc2[...] * (1.0 / _stat(st, 1, ST_L)[...])))
    r = lax.rsqrt(jnp.mean(o * o, axis=0, keepdims=True) + RMS_EPS)
    g = jnp.concatenate([g_ref[...]] * (TQ_A // LANES), axis=1)
    o_ref[...] = (o * r * g * (1.0 - lam_init)).T.astype(BF16)


def _attn_a(slopes, qk, vt, pq, pk, dl, g_b, lam_init):
    s = qk.shape[0]
    return pl.pallas_call(
        functools.partial(_attn_a_kernel, lam_init=lam_init),
        out_shape=jax.ShapeDtypeStruct((s, A_WIDTH), BF16),
        grid=(N_HEADS, s // TQ_A),
        in_specs=[pl.BlockSpec(memory_space=pltpu.SMEM),
                  pl.BlockSpec((TQ_A, HEAD_W), lambda h, i: (i, h)),
                  pl.BlockSpec((s, HEAD_W), lambda h, i: (0, N_HEADS + h)),
                  pl.BlockSpec((s // TK, HEAD_W, TK), lambda h, i: (0, h, 0)),
                  pl.BlockSpec((1, TQ_A), lambda h, i: (0, i)),
                  pl.BlockSpec((s, LANES), lambda h, i: (0, 0)),
                  pl.BlockSpec((4, A_DHALF), lambda h, i: (0, 0)),
                  pl.BlockSpec((HEAD_W, LANES), lambda h, i: (0, 0))],
        out_specs=pl.BlockSpec((TQ_A, HEAD_W), lambda h, i: (i, h)),
        scratch_shapes=[pltpu.VMEM((TK, TQ_A), F32), pltpu.VMEM((TK, TQ_A), F32),
                        pltpu.VMEM((TK, TQ_A), BF16), pltpu.VMEM((TK, TQ_A), BF16),
                        pltpu.VMEM((HEAD_W, TQ_A), F32), pltpu.VMEM((HEAD_W, TQ_A), F32),
                        pltpu.VMEM((2 * ST_ROWS, TQ_A), F32)],
        compiler_params=_cparams(("parallel", "parallel")),
        name="attn_a",
    )(slopes, qk, qk, vt, pq, pk, dl, g_b)


def _attn_c_kernel(q_ref, k_ref, vt_ref, o_ref, s, p, acc, st):
    i = pl.program_id(1)
    q = q_ref[...]
    _online_softmax(i, TQ_C, k_ref, vt_ref, (q,), (s,), (p,), (acc,), st, None)
    o_ref[...] = (acc[...] * (1.0 / _stat(st, 0, ST_L)[...])).T.astype(BF16)


def _attn_c(q, k, vt):
    s = q.shape[0]
    return pl.pallas_call(
        _attn_c_kernel,
        out_shape=jax.ShapeDtypeStruct((s, C_WIDTH), BF16),
        grid=(N_HEADS, s // TQ_C),
        in_specs=[pl.BlockSpec((TQ_C, HEAD_W), lambda h, i: (i, h)),
                  pl.BlockSpec((s, HEAD_W), lambda h, i: (0, h)),
                  pl.BlockSpec((s // TK, HEAD_W, TK), lambda h, i: (0, h, 0))],
        out_specs=pl.BlockSpec((TQ_C, HEAD_W), lambda h, i: (i, h)),
        scratch_shapes=[pltpu.VMEM((TK, TQ_C), F32), pltpu.VMEM((TK, TQ_C), BF16),
                        pltpu.VMEM((HEAD_W, TQ_C), F32), pltpu.VMEM((2 * ST_ROWS, TQ_C), F32)],
        compiler_params=_cparams(("parallel", "parallel")),
        name="attn_c",
    )(q, k, vt)


def _out_ln_kernel(oa_ref, ob_ref, oc_ref, wa_ref, wb_ref, wc_ref, h_ref, g_ref, b_ref,
                   y_ref, yb_ref):
    m = _dot(oa_ref[...], wa_ref[...]) + _dot(ob_ref[...], wb_ref[...]) + _dot(oc_ref[...], wc_ref[...])
    y = _layer_norm(ALPHA * h_ref[...] + m, g_ref[...], b_ref[...])
    y_ref[...] = y
    yb_ref[...] = y.astype(BF16)


def _out_ln(oa, ob, oc, wa, wb, wc, h, g, b):
    s, d = h.shape
    row = pl.BlockSpec((TS, d), lambda i: (i, 0))
    vec = pl.BlockSpec((1, d), lambda i: (0, 0))

    def rows(a):
        return pl.BlockSpec((TS, a.shape[1]), lambda i: (i, 0))

    def full(a):
        return pl.BlockSpec(a.shape, lambda i: (0, 0))

    return pl.pallas_call(
        _out_ln_kernel,
        out_shape=(jax.ShapeDtypeStruct((s, d), F32), jax.ShapeDtypeStruct((s, d), BF16)),
        grid=(s // TS,),
        in_specs=[rows(oa), rows(ob), rows(oc), full(wa), full(wb), full(wc), row, vec, vec],
        out_specs=(row, row),
        compiler_params=_cparams(("parallel",)),
        name="out_ln",
    )(oa, ob, oc, wa, wb, wc, h, g.reshape(1, d), b.reshape(1, d))


def _ffn_kernel(xb_ref, wg_ref, wu_ref, cw_ref, wd_ref, h_ref, g_ref, b_ref, y_ref, yb_ref,
                gbuf, carry, acc):
    i = pl.program_id(0)
    j = pl.program_id(1)
    x = xb_ref[...]

    @pl.when(i == 0)
    def _():
        carry[j] = jnp.zeros((HALO, TF), F32)

    gbuf[pl.ds(0, HALO), :] = carry[j]
    gbuf[pl.ds(HALO, TS), :] = _dot(x, wg_ref[...])
    carry[j] = gbuf[pl.ds(TS, HALO), :]
    gate = _causal_conv3(gbuf, cw_ref, TS)
    act = (gate * (1.0 / (1.0 + jnp.exp(-gate))) * _dot(x, wu_ref[...])).astype(BF16)
    part = _dot(act, wd_ref[...])

    @pl.when(j == 0)
    def _():
        acc[...] = part

    @pl.when(j > 0)
    def _():
        acc[...] += part

    @pl.when(j == pl.num_programs(1) - 1)
    def _():
        y = _layer_norm(ALPHA * h_ref[...] + acc[...], g_ref[...], b_ref[...])
        y_ref[...] = y
        yb_ref[...] = y.astype(BF16)


def _ffn(xb, wg, wu, conv_w, wd, h, g, b):
    s, d = h.shape
    nf = D_FF // TF
    row = pl.BlockSpec((TS, d), lambda i, j: (i, 0))
    vec = pl.BlockSpec((1, d), lambda i, j: (0, 0))
    return pl.pallas_call(
        _ffn_kernel,
        out_shape=(jax.ShapeDtypeStruct((s, d), F32), jax.ShapeDtypeStruct((s, d), BF16)),
        grid=(s // TS, nf),
        in_specs=[row,
                  pl.BlockSpec((d, TF), lambda i, j: (0, j)),
                  pl.BlockSpec((d, TF), lambda i, j: (0, j)),
                  pl.BlockSpec((3, TF), lambda i, j: (0, j)),
                  pl.BlockSpec((TF, d), lambda i, j: (j, 0)),
                  row, vec, vec],
        out_specs=(row, row),
        scratch_shapes=[pltpu.VMEM((HALO + TS, TF), F32),
                        pltpu.VMEM((nf, HALO, TF), F32),
                        pltpu.VMEM((TS, d), F32)],
        compiler_params=_cparams(("arbitrary", "arbitrary")),
        name="ffn",
    )(xb, wg, wu, conv_w, wd, h, g.reshape(1, d), b.reshape(1, d))


def _pad_heads(w, width):
    kdim = w.shape[0]
    w = w.reshape(kdim, N_HEADS, width)
    return jnp.pad(w, ((0, 0), (0, 0), (0, HEAD_W - width))).reshape(kdim, N_HEADS * HEAD_W)


def _prep_layer(w_in, w_uq, w_ukv, w_o):
    aq, ak, av = 0, A_WIDTH, 2 * A_WIDTH
    bh = 3 * A_WIDTH
    cq = bh + 3 * B_WIDTH
    ckv = cq + C_Q_RANK
    ckr = ckv + C_KV_RANK
    kr_tile = jnp.pad(w_in[:, ckr:], ((0, 0), (C_NOPE, HEAD_W - C_NOPE - C_ROPE)))
    ukv = w_ukv.reshape(C_KV_RANK, N_HEADS, C_NOPE + HEAD_W)
    return dict(
        w_qk=w_in[:, aq:av].astype(BF16),
        w_vt=w_in[:, av:bh].T.astype(BF16),
        w_b=w_in[:, bh:cq].astype(BF16),
        w_c=jnp.concatenate([w_in[:, cq:ckr], kr_tile], axis=1).astype(BF16),
        w_uq=_pad_heads(w_uq, C_NOPE + C_ROPE).astype(BF16),
        w_uk=_pad_heads(ukv[:, :, :C_NOPE].reshape(C_KV_RANK, N_HEADS * C_NOPE), C_NOPE).astype(BF16),
        w_uvt=ukv[:, :, C_NOPE:].reshape(C_KV_RANK, C_WIDTH).T.astype(BF16),
        w_oa=w_o[:A_WIDTH].astype(BF16),
        w_ob=w_o[A_WIDTH:A_WIDTH + B_WIDTH].astype(BF16),
        w_oc=w_o[A_WIDTH + B_WIDTH:].astype(BF16),
    )


def kernel(x, positions, ln_in_g, ln_in_b, w_in, diff_lambda, diff_norm_g, conv_w, mla_q_norm_g, mla_kv_norm_g, w_uq, w_ukv, w_o, ln1_g, ln1_b, ffn_w_gate, ffn_w_up, ffn_conv_w, ffn_w_down, ln2_g, ln2_b):
    batch, s, d = x.shape
    assert (batch, s, d) == (1, SEQ, D_MODEL)
    pos = positions.reshape(s)
    pos_f = pos.astype(F32)
    pq = pos_f.reshape(1, s)
    pk = jnp.broadcast_to(pos_f.reshape(s, 1), (s, LANES))
    heads = jnp.arange(N_HEADS, dtype=F32)
    slopes = (2.0 ** (-8.0 * (heads + 1.0) / N_HEADS)) * LOG2E
    qk_scale = jnp.concatenate([jnp.full((1, A_WIDTH), A_DHALF ** -0.5 * LOG2E, F32),
                                jnp.ones((1, A_WIDTH), F32)], axis=1)
    c_scale = (C_NOPE + C_ROPE) ** -0.5 * LOG2E

    cos, sina, sinb = _rope_tables(pos)
    h, hb = _ln_in(x.reshape(s, d), ln_in_g, ln_in_b)
    for l in range(DEPTH):
        lam_init = 0.8 - 0.6 * math.exp(-0.3 * l)
        w = _prep_layer(w_in[l], w_uq[l], w_ukv[l], w_o[l])
        qk = _proj_qk(hb, w["w_qk"], qk_scale)
        vt_a = _proj_vt(hb, w["w_vt"])
        g_b = jnp.broadcast_to(diff_norm_g[l].reshape(HEAD_W, 1), (HEAD_W, LANES))
        o_a = _attn_a(slopes, qk, vt_a, pq, pk, diff_lambda[l], g_b, lam_init)
        o_b = _mix_b(hb, w["w_b"], conv_w[l])
        q_c, k_c, vt_c = _proj_c(hb, w["w_c"], mla_q_norm_g[l].reshape(1, -1), mla_kv_norm_g[l].reshape(1, -1),
                                 w["w_uq"], w["w_uk"], w["w_uvt"], cos, sina, sinb, c_scale)
        o_c = _attn_c(q_c, k_c, vt_c)
        h, hb = _out_ln(o_a, o_b, o_c, w["w_oa"], w["w_ob"], w["w_oc"], h, ln1_g[l], ln1_b[l])
        h, hb = _ffn(hb, ffn_w_gate[l].astype(BF16), ffn_w_up[l].astype(BF16), ffn_conv_w[l],
                     ffn_w_down[l].astype(BF16), h, ln2_g[l], ln2_b[l])
    return h.reshape(batch, s, d)
```

```python
import functools
import math

import jax
import jax.numpy as jnp
from jax import lax
from jax.experimental import pallas as pl
from jax.experimental.pallas import tpu as pltpu

F32 = jnp.float32
BF16 = jnp.bfloat16

D_MODEL = 2048
SEQ = 16384
DEPTH = 2
CHUNK = 64
N_HEADS = 6
HEAD_W = 128
A_DHALF = 64
A_WIDTH = N_HEADS * HEAD_W
B_WIDTH = 512
C_NOPE = 64
C_ROPE = 32
C_WIDTH = N_HEADS * HEAD_W
C_Q_RANK = 512
C_KV_RANK = 256
ROPE_THETA = 10000.0
D_FF = 5632
ALPHA = (2.0 * DEPTH) ** 0.25
LN_EPS = 1e-5
RMS_EPS = 1e-6
NEG = -1e30
LOG2E = math.log2(math.e)

LANES = 128
TS = 512
TQ_A = 512
TQ_C = 512
TK = 512
TF = 512
HALO = 8
VMEM_LIMIT = 56 * 1024 * 1024


def _cparams(sem):
    return pltpu.CompilerParams(dimension_semantics=sem, vmem_limit_bytes=VMEM_LIMIT)


def _layer_norm(x, g, b):
    mu = jnp.mean(x, axis=-1, keepdims=True)
    xc = x - mu
    var = jnp.mean(xc * xc, axis=-1, keepdims=True)
    return xc * lax.rsqrt(var + LN_EPS) * g + b


def _rms_norm(x, g):
    return x * lax.rsqrt(jnp.mean(x * x, axis=-1, keepdims=True) + RMS_EPS) * g


def _dot(a, b):
    return jnp.dot(a, b, preferred_element_type=F32)


def _dot_nt(a, b):
    return lax.dot_general(a, b, (((1,), (1,)), ((), ())), preferred_element_type=F32)


def _ln_in_kernel(x_ref, g_ref, b_ref, h_ref, hb_ref):
    h = _layer_norm(x_ref[...], g_ref[...], b_ref[...])
    h_ref[...] = h
    hb_ref[...] = h.astype(BF16)


def _ln_in(x, g, b):
    s, d = x.shape
    row = pl.BlockSpec((TS, d), lambda i: (i, 0))
    vec = pl.BlockSpec((1, d), lambda i: (0, 0))
    return pl.pallas_call(
        _ln_in_kernel,
        out_shape=(jax.ShapeDtypeStruct((s, d), F32), jax.ShapeDtypeStruct((s, d), BF16)),
        grid=(s // TS,),
        in_specs=[row, vec, vec],
        out_specs=(row, row),
        compiler_params=_cparams(("parallel",)),
        name="ln_in",
    )(x, g.reshape(1, d), b.reshape(1, d))


def _rope_tab_kernel(pos_ref, invf_ref, ma_ref, mb_ref, cos_ref, sina_ref, sinb_ref):
    ang = pos_ref[...].astype(F32) * invf_ref[...]
    sn = jnp.sin(ang)
    cos_ref[...] = jnp.cos(ang)
    sina_ref[...] = sn * ma_ref[...]
    sinb_ref[...] = sn * mb_ref[...]


def _rope_tables(positions):
    s = positions.shape[0]
    half = C_ROPE // 2
    inv_freq = ROPE_THETA ** (-jnp.arange(0, C_ROPE, 2, dtype=F32) / C_ROPE)
    zeros = jnp.zeros((LANES,), F32)
    invf = zeros.at[C_NOPE:C_NOPE + half].set(inv_freq).at[C_NOPE + half:C_NOPE + C_ROPE].set(inv_freq)
    mask_a = zeros.at[C_NOPE + half:C_NOPE + C_ROPE].set(1.0)
    mask_b = zeros.at[C_NOPE:C_NOPE + half].set(-1.0)
    tab = jax.ShapeDtypeStruct((s, LANES), F32)
    row = pl.BlockSpec((TS, LANES), lambda i: (i, 0))
    vec = pl.BlockSpec((1, LANES), lambda i: (0, 0))
    return pl.pallas_call(
        _rope_tab_kernel,
        out_shape=(tab, tab, tab),
        grid=(s // TS,),
        in_specs=[pl.BlockSpec((TS, 1), lambda i: (i, 0)), vec, vec, vec],
        out_specs=(row, row, row),
        compiler_params=_cparams(("parallel",)),
        name="rope_tables",
    )(positions.reshape(s, 1), invf.reshape(1, LANES), mask_a.reshape(1, LANES), mask_b.reshape(1, LANES))


def _rope(x, cos, sina, sinb):
    half = C_ROPE // 2
    return x * cos + pltpu.roll(x, half, 1) * sina + pltpu.roll(x, LANES - half, 1) * sinb


def _proj_qk_kernel(x_ref, w_ref, s_ref, o_ref):
    o_ref[...] = (_dot(x_ref[...], w_ref[...]) * s_ref[...]).astype(BF16)


def _proj_qk(hb, w, col_scale):
    s, d = hb.shape
    n = w.shape[1]
    tn = A_WIDTH
    return pl.pallas_call(
        _proj_qk_kernel,
        out_shape=jax.ShapeDtypeStruct((s, n), BF16),
        grid=(n // tn, s // TS),
        in_specs=[pl.BlockSpec((TS, d), lambda j, i: (i, 0)),
                  pl.BlockSpec((d, tn), lambda j, i: (0, j)),
                  pl.BlockSpec((1, tn), lambda j, i: (0, j))],
        out_specs=pl.BlockSpec((TS, tn), lambda j, i: (i, j)),
        compiler_params=_cparams(("parallel", "parallel")),
        name="proj_qk",
    )(hb, w, col_scale)


def _proj_vt_kernel(x_ref, w_ref, o_ref):
    o_ref[...] = _dot_nt(w_ref[...], x_ref[...]).astype(BF16)


def _proj_vt(hb, w_t):
    s, d = hb.shape
    n = w_t.shape[0]
    return pl.pallas_call(
        _proj_vt_kernel,
        out_shape=jax.ShapeDtypeStruct((s // TK, n, TK), BF16),
        grid=(s // TK,),
        in_specs=[pl.BlockSpec((TK, d), lambda i: (i, 0)),
                  pl.BlockSpec((n, d), lambda i: (0, 0))],
        out_specs=pl.BlockSpec((None, n, TK), lambda i: (i, 0, 0)),
        compiler_params=_cparams(("parallel",)),
        name="proj_vt",
    )(hb, w_t)


def _causal_conv3(buf_ref, cw_ref, rows):
    return (cw_ref[0:1, :] * buf_ref[pl.ds(HALO - 2, rows), :]
            + cw_ref[1:2, :] * buf_ref[pl.ds(HALO - 1, rows), :]
            + cw_ref[2:3, :] * buf_ref[pl.ds(HALO, rows), :])


def _mix_b_kernel(x_ref, w_ref, cw_ref, o_ref, ubuf):
    i = pl.program_id(0)

    @pl.when(i == 0)
    def _():
        ubuf[pl.ds(0, HALO), :] = jnp.zeros((HALO, B_WIDTH), F32)

    p = _dot(x_ref[...], w_ref[...])
    ubuf[pl.ds(HALO, TS), :] = p[:, 2 * B_WIDTH:] * p[:, :B_WIDTH]
    y = _causal_conv3(ubuf, cw_ref, TS)
    o_ref[...] = (p[:, B_WIDTH:2 * B_WIDTH] * y).astype(BF16)
    ubuf[pl.ds(0, HALO), :] = ubuf[pl.ds(TS, HALO), :]


def _mix_b(hb, w, conv_w):
    s, d = hb.shape
    return pl.pallas_call(
        _mix_b_kernel,
        out_shape=jax.ShapeDtypeStruct((s, B_WIDTH), BF16),
        grid=(s // TS,),
        in_specs=[pl.BlockSpec((TS, d), lambda i: (i, 0)),
                  pl.BlockSpec((d, 3 * B_WIDTH), lambda i: (0, 0)),
                  pl.BlockSpec((3, B_WIDTH), lambda i: (0, 0))],
        out_specs=pl.BlockSpec((TS, B_WIDTH), lambda i: (i, 0)),
        scratch_shapes=[pltpu.VMEM((HALO + TS, B_WIDTH), F32)],
        compiler_params=_cparams(("arbitrary",)),
        name="mix_b",
    )(hb, w, conv_w)


def _proj_c_kernel(x_ref, wc_ref, gq_ref, gkv_ref, wuq_ref, wuk_ref, wuvt_ref,
                   cos_ref, sina_ref, sinb_ref, q_ref, k_ref, vt_ref, *, q_scale):
    p = _dot(x_ref[...], wc_ref[...])
    cos, sina, sinb = cos_ref[...], sina_ref[...], sinb_ref[...]
    cq = _rms_norm(p[:, :C_Q_RANK], gq_ref[...]).astype(BF16)
    ckv = _rms_norm(p[:, C_Q_RANK:C_Q_RANK + C_KV_RANK], gkv_ref[...]).astype(BF16)
    k_pe = _rope(p[:, C_Q_RANK + C_KV_RANK:], cos, sina, sinb)
    qc = _dot(cq, wuq_ref[...])
    kc = _dot(ckv, wuk_ref[...])
    for h in range(N_HEADS):
        sl = slice(h * HEAD_W, (h + 1) * HEAD_W)
        q_ref[:, sl] = (_rope(qc[:, sl], cos, sina, sinb) * q_scale).astype(BF16)
        k_ref[:, sl] = (kc[:, sl] + k_pe).astype(BF16)
    vt_ref[...] = _dot_nt(wuvt_ref[...], ckv).astype(BF16)


def _proj_c(hb, wc, gq, gkv, wuq, wuk, wuvt, cos, sina, sinb, q_scale):
    s, d = hb.shape
    nc = wc.shape[1]
    row = pl.BlockSpec((TK, C_WIDTH), lambda i: (i, 0))
    tab = pl.BlockSpec((TK, LANES), lambda i: (i, 0))

    def full(a):
        return pl.BlockSpec(a.shape, lambda i: (0,) * a.ndim)

    return pl.pallas_call(
        functools.partial(_proj_c_kernel, q_scale=q_scale),
        out_shape=(jax.ShapeDtypeStruct((s, C_WIDTH), BF16),
                   jax.ShapeDtypeStruct((s, C_WIDTH), BF16),
                   jax.ShapeDtypeStruct((s // TK, C_WIDTH, TK), BF16)),
        grid=(s // TK,),
        in_specs=[pl.BlockSpec((TK, d), lambda i: (i, 0)), full(wc), full(gq), full(gkv),
                  full(wuq), full(wuk), full(wuvt), tab, tab, tab],
        out_specs=(row, row, pl.BlockSpec((None, C_WIDTH, TK), lambda i: (i, 0, 0))),
        compiler_params=_cparams(("parallel",)),
        name="proj_c",
    )(hb, wc, gq, gkv, wuq, wuk, wuvt, cos, sina, sinb)


ST_M, ST_L, ST_ALPHA, ST_BM, ST_ROWS = 0, 1, 2, 3, 4


def _stat(st_ref, x, row):
    r = x * ST_ROWS + row
    return st_ref.at[r:r + 1, :]


def _online_softmax(i, tq, k_refs, vt_refs, qs, s_refs, p_refs, acc_refs, st_ref, bias_fn):
    n_maps = len(qs)
    n_full = (i * tq) // TK

    def scores(b, masked):
        off = pl.multiple_of(b * TK, TK)
        ks = {id(r): r[pl.ds(off, TK), :] for r in k_refs}
        bias = None if bias_fn is None else bias_fn(off)
        if masked:
            kidx = b * TK + lax.broadcasted_iota(jnp.int32, (TK, tq), 0)
            qidx = i * tq + lax.broadcasted_iota(jnp.int32, (TK, tq), 1)
            mask = (kidx // CHUNK) <= (qidx // CHUNK)
            if bias is not None:
                bias = jnp.where(mask, bias, NEG)
        for x in range(n_maps):
            s = _dot_nt(ks[id(k_refs[x])], qs[x])
            if bias is not None:
                s = s + bias
            elif masked:
                s = jnp.where(mask, s, NEG)
            s_refs[x][...] = s
            _stat(st_ref, x, ST_BM)[...] = jnp.max(s, axis=0, keepdims=True)

    def numerators(x):
        m_old = _stat(st_ref, x, ST_M)[...]
        m_new = jnp.maximum(m_old, _stat(st_ref, x, ST_BM)[...])
        alpha = jnp.exp2(m_old - m_new)
        p = jnp.exp2(s_refs[x][...] - m_new)
        _stat(st_ref, x, ST_L)[...] = alpha * _stat(st_ref, x, ST_L)[...] + jnp.sum(p, axis=0, keepdims=True)
        _stat(st_ref, x, ST_M)[...] = m_new
        _stat(st_ref, x, ST_ALPHA)[...] = alpha
        p_refs[x][...] = p.astype(BF16)

    def values(b):
        vts = {id(r): r[jnp.maximum(b, 0)] for r in vt_refs}
        for x in range(n_maps):
            pv = _dot(vts[id(vt_refs[x])], p_refs[x][...])
            acc_refs[x][...] = acc_refs[x][...] * _stat(st_ref, x, ST_ALPHA)[...] + pv

    def iteration(b, next_scores):
        values(b - 1)
        for x in range(n_maps):
            numerators(x)
        if next_scores is not None:
            scores(b + 1, masked=next_scores == "masked")

    for x in range(n_maps):
        p_refs[x][...] = jnp.zeros_like(p_refs[x])
        acc_refs[x][...] = jnp.zeros_like(acc_refs[x])
        _stat(st_ref, x, ST_M)[...] = jnp.full((1, tq), NEG, F32)
        _stat(st_ref, x, ST_L)[...] = jnp.zeros((1, tq), F32)
        _stat(st_ref, x, ST_ALPHA)[...] = jnp.ones((1, tq), F32)
    scores(0, masked=True)

    def body(b, carry):
        iteration(b, "plain")
        return carry

    lax.fori_loop(0, n_full - 1, body, 0)

    @pl.when(n_full > 0)
    def _():
        iteration(n_full - 1, "masked")

    iteration(n_full, None)
    values(n_full)


def _attn_a_kernel(slope_ref, q_ref, k_ref, vt_ref, pq_ref, pk_ref, dl_ref, g_ref, o_ref,
                   s1, s2, p1, p2, acc1, acc2, st, *, lam_init):
    h = pl.program_id(0)
    i = pl.program_id(1)
    q = q_ref[...]
    lane = lax.broadcasted_iota(jnp.int32, (TQ_A, HEAD_W), 1)
    zero = jnp.zeros_like(q)
    q1 = jnp.where(lane < A_DHALF, q, zero)
    q2 = jnp.where(lane >= A_DHALF, q, zero)
    slope = slope_ref[h]
    aq = pq_ref[...] * slope

    def bias_fn(off):
        bk = pk_ref[pl.ds(off, TK), :] * slope
        d = aq - jnp.concatenate([bk] * (TQ_A // LANES), axis=1)
        return pltpu.bitcast(pltpu.bitcast(d, jnp.uint32) | jnp.uint32(0x80000000), F32)

    _online_softmax(i, TQ_A, (k_ref, k_ref), (vt_ref, vt_ref), (q1, q2), (s1, s2), (p1, p2),
                    (acc1, acc2), st, bias_fn)

    dl = dl_ref[...]
    lam = (jnp.exp(jnp.sum(dl[0:1] * dl[1:2], axis=1, keepdims=True))
           - jnp.exp(jnp.sum(dl[2:3] * dl[3:4], axis=1, keepdims=True)) + lam_init)
    o = (acc1[...] * (1.0 / _stat(st, 0, ST_L)[...])
         - lam * (acc2[...] * (1.0 / _stat(st, 1, ST_L)[...])))
    r = lax.rsqrt(jnp.mean(o * o, axis=0, keepdims=True) + RMS_EPS)
    g = jnp.concatenate([g_ref[...]] * (TQ_A // LANES), axis=1)
    o_ref[...] = (o * r * g * (1.0 - lam_init)).T.astype(BF16)


def _attn_a(slopes, qk, vt, pq, pk, dl, g_b, lam_init):
    s = qk.shape[0]
    return pl.pallas_call(
        functools.partial(_attn_a_kernel, lam_init=lam_init),
        out_shape=jax.ShapeDtypeStruct((s, A_WIDTH), BF16),
        grid=(N_HEADS, s // TQ_A),
        in_specs=[pl.BlockSpec(memory_space=pltpu.SMEM),
                  pl.BlockSpec((TQ_A, HEAD_W), lambda h, i: (i, h)),
                  pl.BlockSpec((s, HEAD_W), lambda h, i: (0, N_HEADS + h)),
                  pl.BlockSpec((s // TK, HEAD_W, TK), lambda h, i: (0, h, 0)),
                  pl.BlockSpec((1, TQ_A), lambda h, i: (0, i)),
                  pl.BlockSpec((s, LANES), lambda h, i: (0, 0)),
                  pl.BlockSpec((4, A_DHALF), lambda h, i: (0, 0)),
                  pl.BlockSpec((HEAD_W, LANES), lambda h, i: (0, 0))],
        out_specs=pl.BlockSpec((TQ_A, HEAD_W), lambda h, i: (i, h)),
        scratch_shapes=[pltpu.VMEM((TK, TQ_A), F32), pltpu.VMEM((TK, TQ_A), F32),
                        pltpu.VMEM((TK, TQ_A), BF16), pltpu.VMEM((TK, TQ_A), BF16),
                        pltpu.VMEM((HEAD_W, TQ_A), F32), pltpu.VMEM((HEAD_W, TQ_A), F32),
                        pltpu.VMEM((2 * ST_ROWS, TQ_A), F32)],
        compiler_params=_cparams(("parallel", "parallel")),
        name="attn_a",
    )(slopes, qk, qk, vt, pq, pk, dl, g_b)


def _attn_c_kernel(qa_ref, qb_ref, ka_ref, kb_ref, vta_ref, vtb_ref, o_ref, sa, sb, pa, pb, acca, accb, st):
    i = pl.program_id(1)
    _online_softmax(i, TQ_C, (ka_ref, kb_ref), (vta_ref, vtb_ref), (qa_ref[...], qb_ref[...]),
                    (sa, sb), (pa, pb), (acca, accb), st, None)
    for x, acc in enumerate((acca, accb)):
        o_ref[:, x * HEAD_W:(x + 1) * HEAD_W] = (acc[...] * (1.0 / _stat(st, x, ST_L)[...])).T.astype(BF16)


def _attn_c(q, k, vt):
    s = q.shape[0]

    def head(x):
        return (pl.BlockSpec((TQ_C, HEAD_W), lambda h, i: (i, 2 * h + x)),
                pl.BlockSpec((s, HEAD_W), lambda h, i: (0, 2 * h + x)),
                pl.BlockSpec((s // TK, HEAD_W, TK), lambda h, i: (0, 2 * h + x, 0)))

    (qa, ka, va), (qb, kb, vb) = head(0), head(1)
    return pl.pallas_call(
        _attn_c_kernel,
        out_shape=jax.ShapeDtypeStruct((s, C_WIDTH), BF16),
        grid=(N_HEADS // 2, s // TQ_C),
        in_specs=[qa, qb, ka, kb, va, vb],
        out_specs=pl.BlockSpec((TQ_C, 2 * HEAD_W), lambda h, i: (i, h)),
        scratch_shapes=[pltpu.VMEM((TK, TQ_C), F32), pltpu.VMEM((TK, TQ_C), F32),
                        pltpu.VMEM((TK, TQ_C), BF16), pltpu.VMEM((TK, TQ_C), BF16),
                        pltpu.VMEM((HEAD_W, TQ_C), F32), pltpu.VMEM((HEAD_W, TQ_C), F32),
                        pltpu.VMEM((2 * ST_ROWS, TQ_C), F32)],
        compiler_params=_cparams(("parallel", "parallel")),
        name="attn_c",
    )(q, q, k, k, vt, vt)


def _out_ln_kernel(oa_ref, ob_ref, oc_ref, wa_ref, wb_ref, wc_ref, h_ref, g_ref, b_ref,
                   y_ref, yb_ref):
    m = _dot(oa_ref[...], wa_ref[...]) + _dot(ob_ref[...], wb_ref[...]) + _dot(oc_ref[...], wc_ref[...])
    y = _layer_norm(ALPHA * h_ref[...] + m, g_ref[...], b_ref[...])
    y_ref[...] = y
    yb_ref[...] = y.astype(BF16)


def _out_ln(oa, ob, oc, wa, wb, wc, h, g, b):
    s, d = h.shape
    row = pl.BlockSpec((TS, d), lambda i: (i, 0))
    vec = pl.BlockSpec((1, d), lambda i: (0, 0))

    def rows(a):
        return pl.BlockSpec((TS, a.shape[1]), lambda i: (i, 0))

    def full(a):
        return pl.BlockSpec(a.shape, lambda i: (0, 0))

    return pl.pallas_call(
        _out_ln_kernel,
        out_shape=(jax.ShapeDtypeStruct((s, d), F32), jax.ShapeDtypeStruct((s, d), BF16)),
        grid=(s // TS,),
        in_specs=[rows(oa), rows(ob), rows(oc), full(wa), full(wb), full(wc), row, vec, vec],
        out_specs=(row, row),
        compiler_params=_cparams(("parallel",)),
        name="out_ln",
    )(oa, ob, oc, wa, wb, wc, h, g.reshape(1, d), b.reshape(1, d))


def _ffn_kernel(xb_ref, wg_ref, wu_ref, cw_ref, wd_ref, h_ref, g_ref, b_ref, y_ref, yb_ref,
                gbuf, carry, acc):
    i = pl.program_id(0)
    j = pl.program_id(1)
    x = xb_ref[...]

    @pl.when(i == 0)
    def _():
        carry[j] = jnp.zeros((HALO, TF), F32)

    gbuf[pl.ds(0, HALO), :] = carry[j]
    gbuf[pl.ds(HALO, TS), :] = _dot(x, wg_ref[...])
    carry[j] = gbuf[pl.ds(TS, HALO), :]
    gate = _causal_conv3(gbuf, cw_ref, TS)
    act = (gate * (1.0 / (1.0 + jnp.exp(-gate))) * _dot(x, wu_ref[...])).astype(BF16)
    part = _dot(act, wd_ref[...])

    @pl.when(j == 0)
    def _():
        acc[...] = part

    @pl.when(j > 0)
    def _():
        acc[...] += part

    @pl.when(j == pl.num_programs(1) - 1)
    def _():
        y = _layer_norm(ALPHA * h_ref[...] + acc[...], g_ref[...], b_ref[...])
        y_ref[...] = y
        yb_ref[...] = y.astype(BF16)


def _ffn(xb, wg, wu, conv_w, wd, h, g, b):
    s, d = h.shape
    nf = D_FF // TF
    row = pl.BlockSpec((TS, d), lambda i, j: (i, 0))
    vec = pl.BlockSpec((1, d), lambda i, j: (0, 0))
    return pl.pallas_call(
        _ffn_kernel,
        out_shape=(jax.ShapeDtypeStruct((s, d), F32), jax.ShapeDtypeStruct((s, d), BF16)),
        grid=(s // TS, nf),
        in_specs=[row,
                  pl.BlockSpec((d, TF), lambda i, j: (0, j)),
                  pl.BlockSpec((d, TF), lambda i, j: (0, j)),
                  pl.BlockSpec((3, TF), lambda i, j: (0, j)),
                  pl.BlockSpec((TF, d), lambda i, j: (j, 0)),
                  row, vec, vec],
        out_specs=(row, row),
        scratch_shapes=[pltpu.VMEM((HALO + TS, TF), F32),
                        pltpu.VMEM((nf, HALO, TF), F32),
                        pltpu.VMEM((TS, d), F32)],
        compiler_params=_cparams(("arbitrary", "arbitrary")),
        name="ffn",
    )(xb, wg, wu, conv_w, wd, h, g.reshape(1, d), b.reshape(1, d))


def _pad_heads(w, width):
    kdim = w.shape[0]
    w = w.reshape(kdim, N_HEADS, width)
    return jnp.pad(w, ((0, 0), (0, 0), (0, HEAD_W - width))).reshape(kdim, N_HEADS * HEAD_W)


def _prep_layer(w_in, w_uq, w_ukv, w_o):
    aq, ak, av = 0, A_WIDTH, 2 * A_WIDTH
    bh = 3 * A_WIDTH
    cq = bh + 3 * B_WIDTH
    ckv = cq + C_Q_RANK
    ckr = ckv + C_KV_RANK
    kr_tile = jnp.pad(w_in[:, ckr:], ((0, 0), (C_NOPE, HEAD_W - C_NOPE - C_ROPE)))
    ukv = w_ukv.reshape(C_KV_RANK, N_HEADS, C_NOPE + HEAD_W)
    return dict(
        w_qk=w_in[:, aq:av].astype(BF16),
        w_vt=w_in[:, av:bh].T.astype(BF16),
        w_b=w_in[:, bh:cq].astype(BF16),
        w_c=jnp.concatenate([w_in[:, cq:ckr], kr_tile], axis=1).astype(BF16),
        w_uq=_pad_heads(w_uq, C_NOPE + C_ROPE).astype(BF16),
        w_uk=_pad_heads(ukv[:, :, :C_NOPE].reshape(C_KV_RANK, N_HEADS * C_NOPE), C_NOPE).astype(BF16),
        w_uvt=ukv[:, :, C_NOPE:].reshape(C_KV_RANK, C_WIDTH).T.astype(BF16),
        w_oa=w_o[:A_WIDTH].astype(BF16),
        w_ob=w_o[A_WIDTH:A_WIDTH + B_WIDTH].astype(BF16),
        w_oc=w_o[A_WIDTH + B_WIDTH:].astype(BF16),
    )


def kernel(x, positions, ln_in_g, ln_in_b, w_in, diff_lambda, diff_norm_g, conv_w, mla_q_norm_g, mla_kv_norm_g, w_uq, w_ukv, w_o, ln1_g, ln1_b, ffn_w_gate, ffn_w_up, ffn_conv_w, ffn_w_down, ln2_g, ln2_b):
    batch, s, d = x.shape
    assert (batch, s, d) == (1, SEQ, D_MODEL)
    pos = positions.reshape(s)
    pos_f = pos.astype(F32)
    pq = pos_f.reshape(1, s)
    pk = jnp.broadcast_to(pos_f.reshape(s, 1), (s, LANES))
    heads = jnp.arange(N_HEADS, dtype=F32)
    slopes = (2.0 ** (-8.0 * (heads + 1.0) / N_HEADS)) * LOG2E
    qk_scale = jnp.concatenate([jnp.full((1, A_WIDTH), A_DHALF ** -0.5 * LOG2E, F32),
                                jnp.ones((1, A_WIDTH), F32)], axis=1)
    c_scale = (C_NOPE + C_ROPE) ** -0.5 * LOG2E

    cos, sina, sinb = _rope_tables(pos)
    h, hb = _ln_in(x.reshape(s, d), ln_in_g, ln_in_b)
    for l in range(DEPTH):
        lam_init = 0.8 - 0.6 * math.exp(-0.3 * l)
        w = _prep_layer(w_in[l], w_uq[l], w_ukv[l], w_o[l])
        qk = _proj_qk(hb, w["w_qk"], qk_scale)
        vt_a = _proj_vt(hb, w["w_vt"])
        g_b = jnp.broadcast_to(diff_norm_g[l].reshape(HEAD_W, 1), (HEAD_W, LANES))
        o_a = _attn_a(slopes, qk, vt_a, pq, pk, diff_lambda[l], g_b, lam_init)
        o_b = _mix_b(hb, w["w_b"], conv_w[l])
        q_c, k_c, vt_c = _proj_c(hb, w["w_c"], mla_q_norm_g[l].reshape(1, -1), mla_kv_norm_g[l].reshape(1, -1),
                                 w["w_uq"], w["w_uk"], w["w_uvt"], cos, sina, sinb, c_scale)
        o_c = _attn_c(q_c, k_c, vt_c)
        h, hb = _out_ln(o_a, o_b, o_c, w["w_oa"], w["w_ob"], w["w_oc"], h, ln1_g[l], ln1_b[l])
        h, hb = _ffn(hb, ffn_w_gate[l].astype(BF16), ffn_w_up[l].astype(BF16), ffn_conv_w[l],
                     ffn_w_down[l].astype(BF16), h, ln2_g[l], ln2_b[l])
    return h.reshape(batch, s, d)
```

```python
import functools
import math

import jax
import jax.numpy as jnp
from jax import lax
from jax.experimental import pallas as pl
from jax.experimental.pallas import tpu as pltpu

F32 = jnp.float32
BF16 = jnp.bfloat16

D_MODEL = 2048
SEQ = 16384
DEPTH = 2
CHUNK = 64
N_HEADS = 6
HEAD_W = 128
BF16_ROWS = 16
VT_ROWS = HEAD_W + BF16_ROWS
A_DHALF = 64
A_WIDTH = N_HEADS * HEAD_W
B_WIDTH = 512
C_NOPE = 64
C_ROPE = 32
C_WIDTH = N_HEADS * HEAD_W
C_Q_RANK = 512
C_KV_RANK = 256
ROPE_THETA = 10000.0
D_FF = 5632
ALPHA = (2.0 * DEPTH) ** 0.25
LN_EPS = 1e-5
RMS_EPS = 1e-6
NEG = -1e30
LOG2E = math.log2(math.e)

LANES = 128
TS = 512
TQ_A = 512
TQ_C = 512
TK = 512
TF = 512
HALO = 8
VMEM_LIMIT = 56 * 1024 * 1024


def _cparams(sem):
    return pltpu.CompilerParams(dimension_semantics=sem, vmem_limit_bytes=VMEM_LIMIT)


def _layer_norm(x, g, b):
    mu = jnp.mean(x, axis=-1, keepdims=True)
    xc = x - mu
    var = jnp.mean(xc * xc, axis=-1, keepdims=True)
    return xc * lax.rsqrt(var + LN_EPS) * g + b


def _rms_norm(x, g):
    return x * lax.rsqrt(jnp.mean(x * x, axis=-1, keepdims=True) + RMS_EPS) * g


def _dot(a, b):
    return jnp.dot(a, b, preferred_element_type=F32)


def _dot_nt(a, b):
    return lax.dot_general(a, b, (((1,), (1,)), ((), ())), preferred_element_type=F32)


def _ln_in_kernel(x_ref, g_ref, b_ref, h_ref, hb_ref):
    h = _layer_norm(x_ref[...], g_ref[...], b_ref[...])
    h_ref[...] = h
    hb_ref[...] = h.astype(BF16)


def _ln_in(x, g, b):
    s, d = x.shape
    row = pl.BlockSpec((TS, d), lambda i: (i, 0))
    vec = pl.BlockSpec((1, d), lambda i: (0, 0))
    return pl.pallas_call(
        _ln_in_kernel,
        out_shape=(jax.ShapeDtypeStruct((s, d), F32), jax.ShapeDtypeStruct((s, d), BF16)),
        grid=(s // TS,),
        in_specs=[row, vec, vec],
        out_specs=(row, row),
        compiler_params=_cparams(("parallel",)),
        name="ln_in",
    )(x, g.reshape(1, d), b.reshape(1, d))


def _rope_tab_kernel(pos_ref, invf_ref, ma_ref, mb_ref, cos_ref, sina_ref, sinb_ref):
    ang = pos_ref[...].astype(F32) * invf_ref[...]
    sn = jnp.sin(ang)
    cos_ref[...] = jnp.cos(ang)
    sina_ref[...] = sn * ma_ref[...]
    sinb_ref[...] = sn * mb_ref[...]


def _rope_tables(positions):
    s = positions.shape[0]
    half = C_ROPE // 2
    inv_freq = ROPE_THETA ** (-jnp.arange(0, C_ROPE, 2, dtype=F32) / C_ROPE)
    zeros = jnp.zeros((LANES,), F32)
    invf = zeros.at[C_NOPE:C_NOPE + half].set(inv_freq).at[C_NOPE + half:C_NOPE + C_ROPE].set(inv_freq)
    mask_a = zeros.at[C_NOPE + half:C_NOPE + C_ROPE].set(1.0)
    mask_b = zeros.at[C_NOPE:C_NOPE + half].set(-1.0)
    tab = jax.ShapeDtypeStruct((s, LANES), F32)
    row = pl.BlockSpec((TS, LANES), lambda i: (i, 0))
    vec = pl.BlockSpec((1, LANES), lambda i: (0, 0))
    return pl.pallas_call(
        _rope_tab_kernel,
        out_shape=(tab, tab, tab),
        grid=(s // TS,),
        in_specs=[pl.BlockSpec((TS, 1), lambda i: (i, 0)), vec, vec, vec],
        out_specs=(row, row, row),
        compiler_params=_cparams(("parallel",)),
        name="rope_tables",
    )(positions.reshape(s, 1), invf.reshape(1, LANES), mask_a.reshape(1, LANES), mask_b.reshape(1, LANES))


def _rope(x, cos, sina, sinb):
    half = C_ROPE // 2
    return x * cos + pltpu.roll(x, half, 1) * sina + pltpu.roll(x, LANES - half, 1) * sinb


def _proj_qk_kernel(x_ref, w_ref, s_ref, o_ref):
    o_ref[...] = (_dot(x_ref[...], w_ref[...]) * s_ref[...]).astype(BF16)


def _proj_qk(hb, w, col_scale):
    s, d = hb.shape
    n = w.shape[1]
    tn = A_WIDTH
    return pl.pallas_call(
        _proj_qk_kernel,
        out_shape=jax.ShapeDtypeStruct((s, n), BF16),
        grid=(n // tn, s // TS),
        in_specs=[pl.BlockSpec((TS, d), lambda j, i: (i, 0)),
                  pl.BlockSpec((d, tn), lambda j, i: (0, j)),
                  pl.BlockSpec((1, tn), lambda j, i: (0, j))],
        out_specs=pl.BlockSpec((TS, tn), lambda j, i: (i, j)),
        compiler_params=_cparams(("parallel", "parallel")),
        name="proj_qk",
    )(hb, w, col_scale)


def _store_vt(vt_ref, vt):
    ones_row = (lax.broadcasted_iota(jnp.int32, (BF16_ROWS, TK), 0) == 0).astype(BF16)
    for h in range(N_HEADS):
        vt_ref[h * VT_ROWS:h * VT_ROWS + HEAD_W, :] = vt[h * HEAD_W:(h + 1) * HEAD_W, :].astype(BF16)
        vt_ref[h * VT_ROWS + HEAD_W:(h + 1) * VT_ROWS, :] = ones_row


def _proj_vt_kernel(x_ref, w_ref, o_ref):
    _store_vt(o_ref, _dot_nt(w_ref[...], x_ref[...]))


def _proj_vt(hb, w_t):
    s, d = hb.shape
    n = w_t.shape[0]
    return pl.pallas_call(
        _proj_vt_kernel,
        out_shape=jax.ShapeDtypeStruct((s // TK, N_HEADS * VT_ROWS, TK), BF16),
        grid=(s // TK,),
        in_specs=[pl.BlockSpec((TK, d), lambda i: (i, 0)),
                  pl.BlockSpec((n, d), lambda i: (0, 0))],
        out_specs=pl.BlockSpec((None, N_HEADS * VT_ROWS, TK), lambda i: (i, 0, 0)),
        compiler_params=_cparams(("parallel",)),
        name="proj_vt",
    )(hb, w_t)


def _causal_conv3(buf_ref, cw_ref, rows):
    return (cw_ref[0:1, :] * buf_ref[pl.ds(HALO - 2, rows), :]
            + cw_ref[1:2, :] * buf_ref[pl.ds(HALO - 1, rows), :]
            + cw_ref[2:3, :] * buf_ref[pl.ds(HALO, rows), :])


def _mix_b_kernel(x_ref, w_ref, cw_ref, o_ref, ubuf):
    i = pl.program_id(0)

    @pl.when(i == 0)
    def _():
        ubuf[pl.ds(0, HALO), :] = jnp.zeros((HALO, B_WIDTH), F32)

    p = _dot(x_ref[...], w_ref[...])
    ubuf[pl.ds(HALO, TS), :] = p[:, 2 * B_WIDTH:] * p[:, :B_WIDTH]
    y = _causal_conv3(ubuf, cw_ref, TS)
    o_ref[...] = (p[:, B_WIDTH:2 * B_WIDTH] * y).astype(BF16)
    ubuf[pl.ds(0, HALO), :] = ubuf[pl.ds(TS, HALO), :]


def _mix_b(hb, w, conv_w):
    s, d = hb.shape
    return pl.pallas_call(
        _mix_b_kernel,
        out_shape=jax.ShapeDtypeStruct((s, B_WIDTH), BF16),
        grid=(s // TS,),
        in_specs=[pl.BlockSpec((TS, d), lambda i: (i, 0)),
                  pl.BlockSpec((d, 3 * B_WIDTH), lambda i: (0, 0)),
                  pl.BlockSpec((3, B_WIDTH), lambda i: (0, 0))],
        out_specs=pl.BlockSpec((TS, B_WIDTH), lambda i: (i, 0)),
        scratch_shapes=[pltpu.VMEM((HALO + TS, B_WIDTH), F32)],
        compiler_params=_cparams(("arbitrary",)),
        name="mix_b",
    )(hb, w, conv_w)


def _proj_c_kernel(x_ref, wc_ref, gq_ref, gkv_ref, wuq_ref, wuk_ref, wuvt_ref,
                   cos_ref, sina_ref, sinb_ref, q_ref, k_ref, vt_ref, *, q_scale):
    p = _dot(x_ref[...], wc_ref[...])
    cos, sina, sinb = cos_ref[...], sina_ref[...], sinb_ref[...]
    cq = _rms_norm(p[:, :C_Q_RANK], gq_ref[...]).astype(BF16)
    ckv = _rms_norm(p[:, C_Q_RANK:C_Q_RANK + C_KV_RANK], gkv_ref[...]).astype(BF16)
    k_pe = _rope(p[:, C_Q_RANK + C_KV_RANK:], cos, sina, sinb)
    qc = _dot(cq, wuq_ref[...])
    kc = _dot(ckv, wuk_ref[...])
    for h in range(N_HEADS):
        sl = slice(h * HEAD_W, (h + 1) * HEAD_W)
        q_ref[:, sl] = (_rope(qc[:, sl], cos, sina, sinb) * q_scale).astype(BF16)
        k_ref[:, sl] = (kc[:, sl] + k_pe).astype(BF16)
    _store_vt(vt_ref, _dot_nt(wuvt_ref[...], ckv))


def _proj_c(hb, wc, gq, gkv, wuq, wuk, wuvt, cos, sina, sinb, q_scale):
    s, d = hb.shape
    nc = wc.shape[1]
    row = pl.BlockSpec((TK, C_WIDTH), lambda i: (i, 0))
    tab = pl.BlockSpec((TK, LANES), lambda i: (i, 0))

    def full(a):
        return pl.BlockSpec(a.shape, lambda i: (0,) * a.ndim)

    return pl.pallas_call(
        functools.partial(_proj_c_kernel, q_scale=q_scale),
        out_shape=(jax.ShapeDtypeStruct((s, C_WIDTH), BF16),
                   jax.ShapeDtypeStruct((s, C_WIDTH), BF16),
                   jax.ShapeDtypeStruct((s // TK, N_HEADS * VT_ROWS, TK), BF16)),
        grid=(s // TK,),
        in_specs=[pl.BlockSpec((TK, d), lambda i: (i, 0)), full(wc), full(gq), full(gkv),
                  full(wuq), full(wuk), full(wuvt), tab, tab, tab],
        out_specs=(row, row, pl.BlockSpec((None, N_HEADS * VT_ROWS, TK), lambda i: (i, 0, 0))),
        compiler_params=_cparams(("parallel",)),
        name="proj_c",
    )(hb, wc, gq, gkv, wuq, wuk, wuvt, cos, sina, sinb)


ST_M, ST_ALPHA, ST_BM, ST_ROWS = 0, 1, 2, 4


def _stat(st_ref, x, row):
    r = x * ST_ROWS + row
    return st_ref.at[r:r + 1, :]


def _online_softmax(i, tq, k_refs, vt_refs, qs, s_refs, p_refs, acc_refs, st_ref, bias_fn):
    n_maps = len(qs)
    n_full = (i * tq) // TK

    def scores(b, masked):
        off = pl.multiple_of(b * TK, TK)
        ks = {id(r): r[pl.ds(off, TK), :] for r in k_refs}
        bias = None if bias_fn is None else bias_fn(off)
        if masked:
            kidx = b * TK + lax.broadcasted_iota(jnp.int32, (TK, tq), 0)
            qidx = i * tq + lax.broadcasted_iota(jnp.int32, (TK, tq), 1)
            mask = (kidx // CHUNK) <= (qidx // CHUNK)
            if bias is not None:
                bias = jnp.where(mask, bias, NEG)
        for x in range(n_maps):
            s = _dot_nt(ks[id(k_refs[x])], qs[x])
            if bias is not None:
                s = s + bias
            elif masked:
                s = jnp.where(mask, s, NEG)
            s_refs[x][...] = s
            _stat(st_ref, x, ST_BM)[...] = jnp.max(s, axis=0, keepdims=True)

    def numerators(x):
        m_old = _stat(st_ref, x, ST_M)[...]
        m_new = jnp.maximum(m_old, _stat(st_ref, x, ST_BM)[...])
        _stat(st_ref, x, ST_M)[...] = m_new
        _stat(st_ref, x, ST_ALPHA)[...] = jnp.exp2(m_old - m_new)
        p_refs[x][...] = jnp.exp2(s_refs[x][...] - m_new).astype(BF16)

    def values(b):
        vts = {id(r): r[jnp.maximum(b, 0)] for r in vt_refs}
        for x in range(n_maps):
            pv = _dot(vts[id(vt_refs[x])], p_refs[x][...])
            acc_refs[x][...] = acc_refs[x][...] * _stat(st_ref, x, ST_ALPHA)[...] + pv

    def iteration(b, next_scores):
        values(b - 1)
        for x in range(n_maps):
            numerators(x)
        if next_scores is not None:
            scores(b + 1, masked=next_scores == "masked")

    for x in range(n_maps):
        p_refs[x][...] = jnp.zeros_like(p_refs[x])
        acc_refs[x][...] = jnp.zeros_like(acc_refs[x])
        _stat(st_ref, x, ST_M)[...] = jnp.full((1, tq), NEG, F32)
        _stat(st_ref, x, ST_ALPHA)[...] = jnp.ones((1, tq), F32)
    scores(0, masked=True)

    n_plain = jnp.maximum(n_full - 1, 0)

    def pair(t, carry):
        iteration(2 * t, "plain")
        iteration(2 * t + 1, "plain")
        return carry

    lax.fori_loop(0, n_plain // 2, pair, 0)

    @pl.when(n_plain % 2 == 1)
    def _():
        iteration(n_plain - 1, "plain")

    @pl.when(n_full > 0)
    def _():
        iteration(n_full - 1, "masked")

    iteration(n_full, None)
    values(n_full)


def _normalised(acc_ref):
    return acc_ref[0:HEAD_W, :] * (1.0 / acc_ref[HEAD_W:HEAD_W + 1, :])


def _attn_a_kernel(slope_ref, q_ref, k_ref, vt_ref, pq_ref, pk_ref, dl_ref, g_ref, o_ref,
                   s1, s2, p1, p2, acc1, acc2, st, *, lam_init):
    h = pl.program_id(0)
    i = pl.program_id(1)
    q = q_ref[...]
    lane = lax.broadcasted_iota(jnp.int32, (TQ_A, HEAD_W), 1)
    zero = jnp.zeros_like(q)
    q1 = jnp.where(lane < A_DHALF, q, zero)
    q2 = jnp.where(lane >= A_DHALF, q, zero)
    slope = slope_ref[h]
    aq = pq_ref[...] * slope

    def bias_fn(off):
        bk = pk_ref[pl.ds(off, TK), :] * slope
        d = aq - jnp.concatenate([bk] * (TQ_A // LANES), axis=1)
        return pltpu.bitcast(pltpu.bitcast(d, jnp.uint32) | jnp.uint32(0x80000000), F32)

    _online_softmax(i, TQ_A, (k_ref, k_ref), (vt_ref, vt_ref), (q1, q2), (s1, s2), (p1, p2),
                    (acc1, acc2), st, bias_fn)

    dl = dl_ref[...]
    lam = (jnp.exp(jnp.sum(dl[0:1] * dl[1:2], axis=1, keepdims=True))
           - jnp.exp(jnp.sum(dl[2:3] * dl[3:4], axis=1, keepdims=True)) + lam_init)
    o = _normalised(acc1) - lam * _normalised(acc2)
    r = lax.rsqrt(jnp.mean(o * o, axis=0, keepdims=True) + RMS_EPS)
    g = jnp.concatenate([g_ref[...]] * (TQ_A // LANES), axis=1)
    o_ref[...] = (o * r * g * (1.0 - lam_init)).T.astype(BF16)


def _attn_a(slopes, qk, vt, pq, pk, dl, g_b, lam_init):
    s = qk.shape[0]
    return pl.pallas_call(
        functools.partial(_attn_a_kernel, lam_init=lam_init),
        out_shape=jax.ShapeDtypeStruct((s, A_WIDTH), BF16),
        grid=(N_HEADS, s // TQ_A),
        in_specs=[pl.BlockSpec(memory_space=pltpu.SMEM),
                  pl.BlockSpec((TQ_A, HEAD_W), lambda h, i: (i, h)),
                  pl.BlockSpec((s, HEAD_W), lambda h, i: (0, N_HEADS + h)),
                  pl.BlockSpec((s // TK, VT_ROWS, TK), lambda h, i: (0, h, 0)),
                  pl.BlockSpec((1, TQ_A), lambda h, i: (0, i)),
                  pl.BlockSpec((s, LANES), lambda h, i: (0, 0)),
                  pl.BlockSpec((4, A_DHALF), lambda h, i: (0, 0)),
                  pl.BlockSpec((HEAD_W, LANES), lambda h, i: (0, 0))],
        out_specs=pl.BlockSpec((TQ_A, HEAD_W), lambda h, i: (i, h)),
        scratch_shapes=[pltpu.VMEM((TK, TQ_A), F32), pltpu.VMEM((TK, TQ_A), F32),
                        pltpu.VMEM((TK, TQ_A), BF16), pltpu.VMEM((TK, TQ_A), BF16),
                        pltpu.VMEM((VT_ROWS, TQ_A), F32), pltpu.VMEM((VT_ROWS, TQ_A), F32),
                        pltpu.VMEM((2 * ST_ROWS, TQ_A), F32)],
        compiler_params=_cparams(("parallel", "parallel")),
        name="attn_a",
    )(slopes, qk, qk, vt, pq, pk, dl, g_b)


def _attn_c_kernel(qa_ref, qb_ref, ka_ref, kb_ref, vta_ref, vtb_ref, o_ref, sa, sb, pa, pb, acca, accb, st):
    i = pl.program_id(1)
    _online_softmax(i, TQ_C, (ka_ref, kb_ref), (vta_ref, vtb_ref), (qa_ref[...], qb_ref[...]),
                    (sa, sb), (pa, pb), (acca, accb), st, None)
    for x, acc in enumerate((acca, accb)):
        o_ref[:, x * HEAD_W:(x + 1) * HEAD_W] = _normalised(acc).T.astype(BF16)


def _attn_c(q, k, vt):
    s = q.shape[0]

    def head(x):
        return (pl.BlockSpec((TQ_C, HEAD_W), lambda h, i: (i, 2 * h + x)),
                pl.BlockSpec((s, HEAD_W), lambda h, i: (0, 2 * h + x)),
                pl.BlockSpec((s // TK, VT_ROWS, TK), lambda h, i: (0, 2 * h + x, 0)))

    (qa, ka, va), (qb, kb, vb) = head(0), head(1)
    return pl.pallas_call(
        _attn_c_kernel,
        out_shape=jax.ShapeDtypeStruct((s, C_WIDTH), BF16),
        grid=(N_HEADS // 2, s // TQ_C),
        in_specs=[qa, qb, ka, kb, va, vb],
        out_specs=pl.BlockSpec((TQ_C, 2 * HEAD_W), lambda h, i: (i, h)),
        scratch_shapes=[pltpu.VMEM((TK, TQ_C), F32), pltpu.VMEM((TK, TQ_C), F32),
                        pltpu.VMEM((TK, TQ_C), BF16), pltpu.VMEM((TK, TQ_C), BF16),
                        pltpu.VMEM((VT_ROWS, TQ_C), F32), pltpu.VMEM((VT_ROWS, TQ_C), F32),
                        pltpu.VMEM((2 * ST_ROWS, TQ_C), F32)],
        compiler_params=_cparams(("parallel", "parallel")),
        name="attn_c",
    )(q, q, k, k, vt, vt)


def _out_ln_kernel(oa_ref, ob_ref, oc_ref, wa_ref, wb_ref, wc_ref, h_ref, g_ref, b_ref,
                   y_ref, yb_ref):
    m = _dot(oa_ref[...], wa_ref[...]) + _dot(ob_ref[...], wb_ref[...]) + _dot(oc_ref[...], wc_ref[...])
    y = _layer_norm(ALPHA * h_ref[...] + m, g_ref[...], b_ref[...])
    y_ref[...] = y
    yb_ref[...] = y.astype(BF16)


def _out_ln(oa, ob, oc, wa, wb, wc, h, g, b):
    s, d = h.shape
    row = pl.BlockSpec((TS, d), lambda i: (i, 0))
    vec = pl.BlockSpec((1, d), lambda i: (0, 0))

    def rows(a):
        return pl.BlockSpec((TS, a.shape[1]), lambda i: (i, 0))

    def full(a):
        return pl.BlockSpec(a.shape, lambda i: (0, 0))

    return pl.pallas_call(
        _out_ln_kernel,
        out_shape=(jax.ShapeDtypeStruct((s, d), F32), jax.ShapeDtypeStruct((s, d), BF16)),
        grid=(s // TS,),
        in_specs=[rows(oa), rows(ob), rows(oc), full(wa), full(wb), full(wc), row, vec, vec],
        out_specs=(row, row),
        compiler_params=_cparams(("parallel",)),
        name="out_ln",
    )(oa, ob, oc, wa, wb, wc, h, g.reshape(1, d), b.reshape(1, d))


def _ffn_kernel(xb_ref, wg_ref, wu_ref, cw_ref, wd_ref, h_ref, g_ref, b_ref, y_ref, yb_ref,
                gbuf, carry, acc):
    i = pl.program_id(0)
    j = pl.program_id(1)
    x = xb_ref[...]

    @pl.when(i == 0)
    def _():
        carry[j] = jnp.zeros((HALO, TF), F32)

    gbuf[pl.ds(0, HALO), :] = carry[j]
    gbuf[pl.ds(HALO, TS), :] = _dot(x, wg_ref[...])
    carry[j] = gbuf[pl.ds(TS, HALO), :]
    gate = _causal_conv3(gbuf, cw_ref, TS)
    act = (gate * (1.0 / (1.0 + jnp.exp(-gate))) * _dot(x, wu_ref[...])).astype(BF16)
    part = _dot(act, wd_ref[...])

    @pl.when(j == 0)
    def _():
        acc[...] = part

    @pl.when(j > 0)
    def _():
        acc[...] += part

    @pl.when(j == pl.num_programs(1) - 1)
    def _():
        y = _layer_norm(ALPHA * h_ref[...] + acc[...], g_ref[...], b_ref[...])
        y_ref[...] = y
        yb_ref[...] = y.astype(BF16)


def _ffn(xb, wg, wu, conv_w, wd, h, g, b):
    s, d = h.shape
    nf = D_FF // TF
    row = pl.BlockSpec((TS, d), lambda i, j: (i, 0))
    vec = pl.BlockSpec((1, d), lambda i, j: (0, 0))
    return pl.pallas_call(
        _ffn_kernel,
        out_shape=(jax.ShapeDtypeStruct((s, d), F32), jax.ShapeDtypeStruct((s, d), BF16)),
        grid=(s // TS, nf),
        in_specs=[row,
                  pl.BlockSpec((d, TF), lambda i, j: (0, j)),
                  pl.BlockSpec((d, TF), lambda i, j: (0, j)),
                  pl.BlockSpec((3, TF), lambda i, j: (0, j)),
                  pl.BlockSpec((TF, d), lambda i, j: (j, 0)),
                  row, vec, vec],
        out_specs=(row, row),
        scratch_shapes=[pltpu.VMEM((HALO + TS, TF), F32),
                        pltpu.VMEM((nf, HALO, TF), F32),
                        pltpu.VMEM((TS, d), F32)],
        compiler_params=_cparams(("arbitrary", "arbitrary")),
        name="ffn",
    )(xb, wg, wu, conv_w, wd, h, g.reshape(1, d), b.reshape(1, d))


def _pad_heads(w, width):
    kdim = w.shape[0]
    w = w.reshape(kdim, N_HEADS, width)
    return jnp.pad(w, ((0, 0), (0, 0), (0, HEAD_W - width))).reshape(kdim, N_HEADS * HEAD_W)


def _prep_layer(w_in, w_uq, w_ukv, w_o):
    aq, ak, av = 0, A_WIDTH, 2 * A_WIDTH
    bh = 3 * A_WIDTH
    cq = bh + 3 * B_WIDTH
    ckv = cq + C_Q_RANK
    ckr = ckv + C_KV_RANK
    kr_tile = jnp.pad(w_in[:, ckr:], ((0, 0), (C_NOPE, HEAD_W - C_NOPE - C_ROPE)))
    ukv = w_ukv.reshape(C_KV_RANK, N_HEADS, C_NOPE + HEAD_W)
    return dict(
        w_qk=w_in[:, aq:av].astype(BF16),
        w_vt=w_in[:, av:bh].T.astype(BF16),
        w_b=w_in[:, bh:cq].astype(BF16),
        w_c=jnp.concatenate([w_in[:, cq:ckr], kr_tile], axis=1).astype(BF16),
        w_uq=_pad_heads(w_uq, C_NOPE + C_ROPE).astype(BF16),
        w_uk=_pad_heads(ukv[:, :, :C_NOPE].reshape(C_KV_RANK, N_HEADS * C_NOPE), C_NOPE).astype(BF16),
        w_uvt=ukv[:, :, C_NOPE:].reshape(C_KV_RANK, C_WIDTH).T.astype(BF16),
        w_oa=w_o[:A_WIDTH].astype(BF16),
        w_ob=w_o[A_WIDTH:A_WIDTH + B_WIDTH].astype(BF16),
        w_oc=w_o[A_WIDTH + B_WIDTH:].astype(BF16),
    )


def kernel(x, positions, ln_in_g, ln_in_b, w_in, diff_lambda, diff_norm_g, conv_w, mla_q_norm_g, mla_kv_norm_g, w_uq, w_ukv, w_o, ln1_g, ln1_b, ffn_w_gate, ffn_w_up, ffn_conv_w, ffn_w_down, ln2_g, ln2_b):
    batch, s, d = x.shape
    assert (batch, s, d) == (1, SEQ, D_MODEL)
    pos = positions.reshape(s)
    pos_f = pos.astype(F32)
    pq = pos_f.reshape(1, s)
    pk = jnp.broadcast_to(pos_f.reshape(s, 1), (s, LANES))
    heads = jnp.arange(N_HEADS, dtype=F32)
    slopes = (2.0 ** (-8.0 * (heads + 1.0) / N_HEADS)) * LOG2E
    qk_scale = jnp.concatenate([jnp.full((1, A_WIDTH), A_DHALF ** -0.5 * LOG2E, F32),
                                jnp.ones((1, A_WIDTH), F32)], axis=1)
    c_scale = (C_NOPE + C_ROPE) ** -0.5 * LOG2E

    cos, sina, sinb = _rope_tables(pos)
    h, hb = _ln_in(x.reshape(s, d), ln_in_g, ln_in_b)
    for l in range(DEPTH):
        lam_init = 0.8 - 0.6 * math.exp(-0.3 * l)
        w = _prep_layer(w_in[l], w_uq[l], w_ukv[l], w_o[l])
        qk = _proj_qk(hb, w["w_qk"], qk_scale)
        vt_a = _proj_vt(hb, w["w_vt"])
        g_b = jnp.broadcast_to(diff_norm_g[l].reshape(HEAD_W, 1), (HEAD_W, LANES))
        o_a = _attn_a(slopes, qk, vt_a, pq, pk, diff_lambda[l], g_b, lam_init)
        o_b = _mix_b(hb, w["w_b"], conv_w[l])
        q_c, k_c, vt_c = _proj_c(hb, w["w_c"], mla_q_norm_g[l].reshape(1, -1), mla_kv_norm_g[l].reshape(1, -1),
                                 w["w_uq"], w["w_uk"], w["w_uvt"], cos, sina, sinb, c_scale)
        o_c = _attn_c(q_c, k_c, vt_c)
        h, hb = _out_ln(o_a, o_b, o_c, w["w_oa"], w["w_ob"], w["w_oc"], h, ln1_g[l], ln1_b[l])
        h, hb = _ffn(hb, ffn_w_gate[l].astype(BF16), ffn_w_up[l].astype(BF16), ffn_conv_w[l],
                     ffn_w_down[l].astype(BF16), h, ln2_g[l], ln2_b[l])
    return h.reshape(batch, s, d)
```

```python
import functools
import math

import jax
import jax.numpy as jnp
from jax import lax
from jax.experimental import pallas as pl
from jax.experimental.pallas import tpu as pltpu

F32 = jnp.float32
BF16 = jnp.bfloat16

D_MODEL = 2048
SEQ = 16384
DEPTH = 2
CHUNK = 64
N_HEADS = 6
HEAD_W = 128
BF16_ROWS = 16
VT_ROWS = HEAD_W + BF16_ROWS
A_DHALF = 64
A_WIDTH = N_HEADS * HEAD_W
B_WIDTH = 512
C_NOPE = 64
C_ROPE = 32
C_WIDTH = N_HEADS * HEAD_W
C_Q_RANK = 512
C_KV_RANK = 256
ROPE_THETA = 10000.0
D_FF = 5632
ALPHA = (2.0 * DEPTH) ** 0.25
LN_EPS = 1e-5
RMS_EPS = 1e-6
NEG = -1e30
LOG2E = math.log2(math.e)
ALIBI_SKIP_LOG2 = 192.0
NORM_SLACK = 1.02

LANES = 128
TS = 512
TQ_A = 512
TQ_C = 512
TK = 512
TF = 512
HALO = 8
VMEM_LIMIT = 56 * 1024 * 1024


def _cparams(sem):
    return pltpu.CompilerParams(dimension_semantics=sem, vmem_limit_bytes=VMEM_LIMIT)


def _layer_norm(x, g, b):
    mu = jnp.mean(x, axis=-1, keepdims=True)
    xc = x - mu
    var = jnp.mean(xc * xc, axis=-1, keepdims=True)
    return xc * lax.rsqrt(var + LN_EPS) * g + b


def _rms_norm(x, g):
    return x * lax.rsqrt(jnp.mean(x * x, axis=-1, keepdims=True) + RMS_EPS) * g


def _dot(a, b):
    return jnp.dot(a, b, preferred_element_type=F32)


def _dot_nt(a, b):
    return lax.dot_general(a, b, (((1,), (1,)), ((), ())), preferred_element_type=F32)


def _ln_in_kernel(x_ref, g_ref, b_ref, h_ref, hb_ref):
    h = _layer_norm(x_ref[...], g_ref[...], b_ref[...])
    h_ref[...] = h
    hb_ref[...] = h.astype(BF16)


def _ln_in(x, g, b):
    s, d = x.shape
    row = pl.BlockSpec((TS, d), lambda i: (i, 0))
    vec = pl.BlockSpec((1, d), lambda i: (0, 0))
    return pl.pallas_call(
        _ln_in_kernel,
        out_shape=(jax.ShapeDtypeStruct((s, d), F32), jax.ShapeDtypeStruct((s, d), BF16)),
        grid=(s // TS,),
        in_specs=[row, vec, vec],
        out_specs=(row, row),
        compiler_params=_cparams(("parallel",)),
        name="ln_in",
    )(x, g.reshape(1, d), b.reshape(1, d))


def _rope_tab_kernel(pos_ref, invf_ref, ma_ref, mb_ref, cos_ref, sina_ref, sinb_ref):
    ang = pos_ref[...].astype(F32) * invf_ref[...]
    sn = jnp.sin(ang)
    cos_ref[...] = jnp.cos(ang)
    sina_ref[...] = sn * ma_ref[...]
    sinb_ref[...] = sn * mb_ref[...]


def _rope_tables(positions):
    s = positions.shape[0]
    half = C_ROPE // 2
    inv_freq = ROPE_THETA ** (-jnp.arange(0, C_ROPE, 2, dtype=F32) / C_ROPE)
    zeros = jnp.zeros((LANES,), F32)
    invf = zeros.at[C_NOPE:C_NOPE + half].set(inv_freq).at[C_NOPE + half:C_NOPE + C_ROPE].set(inv_freq)
    mask_a = zeros.at[C_NOPE + half:C_NOPE + C_ROPE].set(1.0)
    mask_b = zeros.at[C_NOPE:C_NOPE + half].set(-1.0)
    tab = jax.ShapeDtypeStruct((s, LANES), F32)
    row = pl.BlockSpec((TS, LANES), lambda i: (i, 0))
    vec = pl.BlockSpec((1, LANES), lambda i: (0, 0))
    return pl.pallas_call(
        _rope_tab_kernel,
        out_shape=(tab, tab, tab),
        grid=(s // TS,),
        in_specs=[pl.BlockSpec((TS, 1), lambda i: (i, 0)), vec, vec, vec],
        out_specs=(row, row, row),
        compiler_params=_cparams(("parallel",)),
        name="rope_tables",
    )(positions.reshape(s, 1), invf.reshape(1, LANES), mask_a.reshape(1, LANES), mask_b.reshape(1, LANES))


def _rope(x, cos, sina, sinb):
    half = C_ROPE // 2
    return x * cos + pltpu.roll(x, half, 1) * sina + pltpu.roll(x, LANES - half, 1) * sinb


def _proj_qk_kernel(x_ref, w_ref, s_ref, e_ref, o_ref, n_ref):
    o = (_dot(x_ref[...], w_ref[...]) * s_ref[...]).astype(BF16)
    o_ref[...] = o
    of = o.astype(F32)
    seg = _dot((of * of).astype(BF16), e_ref[...])
    n_ref[...] = jnp.broadcast_to(jnp.max(seg, axis=0, keepdims=True), (8, LANES))


def _proj_qk(hb, w, col_scale):
    s, d = hb.shape
    n = w.shape[1]
    tn = A_WIDTH
    seg_of_col = jnp.arange(tn, dtype=jnp.int32) // A_DHALF
    e = (seg_of_col[:, None] == jnp.arange(LANES, dtype=jnp.int32)[None, :]).astype(BF16)
    return pl.pallas_call(
        _proj_qk_kernel,
        out_shape=(jax.ShapeDtypeStruct((s, n), BF16),
                   jax.ShapeDtypeStruct((n // tn, s // TS, 8, LANES), F32)),
        grid=(n // tn, s // TS),
        in_specs=[pl.BlockSpec((TS, d), lambda j, i: (i, 0)),
                  pl.BlockSpec((d, tn), lambda j, i: (0, j)),
                  pl.BlockSpec((1, tn), lambda j, i: (0, j)),
                  pl.BlockSpec((tn, LANES), lambda j, i: (0, 0))],
        out_specs=(pl.BlockSpec((TS, tn), lambda j, i: (i, j)),
                   pl.BlockSpec((None, None, 8, LANES), lambda j, i: (j, i, 0, 0))),
        compiler_params=_cparams(("parallel", "parallel")),
        name="proj_qk",
    )(hb, w, col_scale, e)


def _store_vt(vt_ref, vt):
    ones_row = (lax.broadcasted_iota(jnp.int32, (BF16_ROWS, TK), 0) == 0).astype(BF16)
    for h in range(N_HEADS):
        vt_ref[h * VT_ROWS:h * VT_ROWS + HEAD_W, :] = vt[h * HEAD_W:(h + 1) * HEAD_W, :].astype(BF16)
        vt_ref[h * VT_ROWS + HEAD_W:(h + 1) * VT_ROWS, :] = ones_row


def _proj_vt_kernel(x_ref, w_ref, o_ref):
    _store_vt(o_ref, _dot_nt(w_ref[...], x_ref[...]))


def _proj_vt(hb, w_t):
    s, d = hb.shape
    n = w_t.shape[0]
    return pl.pallas_call(
        _proj_vt_kernel,
        out_shape=jax.ShapeDtypeStruct((s // TK, N_HEADS * VT_ROWS, TK), BF16),
        grid=(s // TK,),
        in_specs=[pl.BlockSpec((TK, d), lambda i: (i, 0)),
                  pl.BlockSpec((n, d), lambda i: (0, 0))],
        out_specs=pl.BlockSpec((None, N_HEADS * VT_ROWS, TK), lambda i: (i, 0, 0)),
        compiler_params=_cparams(("parallel",)),
        name="proj_vt",
    )(hb, w_t)


def _causal_conv3(buf_ref, cw_ref, rows):
    return (cw_ref[0:1, :] * buf_ref[pl.ds(HALO - 2, rows), :]
            + cw_ref[1:2, :] * buf_ref[pl.ds(HALO - 1, rows), :]
            + cw_ref[2:3, :] * buf_ref[pl.ds(HALO, rows), :])


def _mix_b_kernel(x_ref, w_ref, cw_ref, o_ref, ubuf):
    i = pl.program_id(0)

    @pl.when(i == 0)
    def _():
        ubuf[pl.ds(0, HALO), :] = jnp.zeros((HALO, B_WIDTH), F32)

    p = _dot(x_ref[...], w_ref[...])
    ubuf[pl.ds(HALO, TS), :] = p[:, 2 * B_WIDTH:] * p[:, :B_WIDTH]
    y = _causal_conv3(ubuf, cw_ref, TS)
    o_ref[...] = (p[:, B_WIDTH:2 * B_WIDTH] * y).astype(BF16)
    ubuf[pl.ds(0, HALO), :] = ubuf[pl.ds(TS, HALO), :]


def _mix_b(hb, w, conv_w):
    s, d = hb.shape
    return pl.pallas_call(
        _mix_b_kernel,
        out_shape=jax.ShapeDtypeStruct((s, B_WIDTH), BF16),
        grid=(s // TS,),
        in_specs=[pl.BlockSpec((TS, d), lambda i: (i, 0)),
                  pl.BlockSpec((d, 3 * B_WIDTH), lambda i: (0, 0)),
                  pl.BlockSpec((3, B_WIDTH), lambda i: (0, 0))],
        out_specs=pl.BlockSpec((TS, B_WIDTH), lambda i: (i, 0)),
        scratch_shapes=[pltpu.VMEM((HALO + TS, B_WIDTH), F32)],
        compiler_params=_cparams(("arbitrary",)),
        name="mix_b",
    )(hb, w, conv_w)


def _proj_c_kernel(x_ref, wc_ref, gq_ref, gkv_ref, wuq_ref, wuk_ref, wuvt_ref,
                   cos_ref, sina_ref, sinb_ref, q_ref, k_ref, vt_ref, *, q_scale):
    p = _dot(x_ref[...], wc_ref[...])
    cos, sina, sinb = cos_ref[...], sina_ref[...], sinb_ref[...]
    cq = _rms_norm(p[:, :C_Q_RANK], gq_ref[...]).astype(BF16)
    ckv = _rms_norm(p[:, C_Q_RANK:C_Q_RANK + C_KV_RANK], gkv_ref[...]).astype(BF16)
    k_pe = _rope(p[:, C_Q_RANK + C_KV_RANK:], cos, sina, sinb)
    qc = _dot(cq, wuq_ref[...])
    kc = _dot(ckv, wuk_ref[...])
    for h in range(N_HEADS):
        sl = slice(h * HEAD_W, (h + 1) * HEAD_W)
        q_ref[:, sl] = (_rope(qc[:, sl], cos, sina, sinb) * q_scale).astype(BF16)
        k_ref[:, sl] = (kc[:, sl] + k_pe).astype(BF16)
    _store_vt(vt_ref, _dot_nt(wuvt_ref[...], ckv))


def _proj_c(hb, wc, gq, gkv, wuq, wuk, wuvt, cos, sina, sinb, q_scale):
    s, d = hb.shape
    nc = wc.shape[1]
    row = pl.BlockSpec((TK, C_WIDTH), lambda i: (i, 0))
    tab = pl.BlockSpec((TK, LANES), lambda i: (i, 0))

    def full(a):
        return pl.BlockSpec(a.shape, lambda i: (0,) * a.ndim)

    return pl.pallas_call(
        functools.partial(_proj_c_kernel, q_scale=q_scale),
        out_shape=(jax.ShapeDtypeStruct((s, C_WIDTH), BF16),
                   jax.ShapeDtypeStruct((s, C_WIDTH), BF16),
                   jax.ShapeDtypeStruct((s // TK, N_HEADS * VT_ROWS, TK), BF16)),
        grid=(s // TK,),
        in_specs=[pl.BlockSpec((TK, d), lambda i: (i, 0)), full(wc), full(gq), full(gkv),
                  full(wuq), full(wuk), full(wuvt), tab, tab, tab],
        out_specs=(row, row, pl.BlockSpec((None, N_HEADS * VT_ROWS, TK), lambda i: (i, 0, 0))),
        compiler_params=_cparams(("parallel",)),
        name="proj_c",
    )(hb, wc, gq, gkv, wuq, wuk, wuvt, cos, sina, sinb)


ST_M, ST_ALPHA, ST_BM, ST_ROWS = 0, 1, 2, 4


def _stat(st_ref, x, row):
    r = x * ST_ROWS + row
    return st_ref.at[r:r + 1, :]


def _online_softmax(i, tq, k_refs, vt_refs, qs, s_refs, p_refs, acc_refs, st_ref, bias_fn, b0=0):
    n_maps = len(qs)
    n_full = (i * tq) // TK

    def scores(b, masked):
        off = pl.multiple_of(b * TK, TK)
        ks = {id(r): r[pl.ds(off, TK), :] for r in k_refs}
        bias = None if bias_fn is None else bias_fn(off)
        if masked:
            kidx = b * TK + lax.broadcasted_iota(jnp.int32, (TK, tq), 0)
            qidx = i * tq + lax.broadcasted_iota(jnp.int32, (TK, tq), 1)
            mask = (kidx // CHUNK) <= (qidx // CHUNK)
            if bias is not None:
                bias = jnp.where(mask, bias, NEG)
        for x in range(n_maps):
            s = _dot_nt(ks[id(k_refs[x])], qs[x])
            if bias is not None:
                s = s + bias
            elif masked:
                s = jnp.where(mask, s, NEG)
            s_refs[x][...] = s
            _stat(st_ref, x, ST_BM)[...] = jnp.max(s, axis=0, keepdims=True)

    def numerators(x):
        m_old = _stat(st_ref, x, ST_M)[...]
        m_new = jnp.maximum(m_old, _stat(st_ref, x, ST_BM)[...])
        _stat(st_ref, x, ST_M)[...] = m_new
        _stat(st_ref, x, ST_ALPHA)[...] = jnp.exp2(m_old - m_new)
        p_refs[x][...] = jnp.exp2(s_refs[x][...] - m_new).astype(BF16)

    def values(b):
        vts = {id(r): r[jnp.maximum(b, 0)] for r in vt_refs}
        for x in range(n_maps):
            pv = _dot(vts[id(vt_refs[x])], p_refs[x][...])
            acc_refs[x][...] = acc_refs[x][...] * _stat(st_ref, x, ST_ALPHA)[...] + pv

    def iteration(b, next_scores):
        values(b - 1)
        for x in range(n_maps):
            numerators(x)
        if next_scores is not None:
            scores(b + 1, masked=next_scores == "masked")

    for x in range(n_maps):
        p_refs[x][...] = jnp.zeros_like(p_refs[x])
        acc_refs[x][...] = jnp.zeros_like(acc_refs[x])
        _stat(st_ref, x, ST_M)[...] = jnp.full((1, tq), NEG, F32)
        _stat(st_ref, x, ST_ALPHA)[...] = jnp.ones((1, tq), F32)
    scores(b0, masked=True)

    n_plain = jnp.maximum(n_full - b0 - 1, 0)

    def pair(t, carry):
        iteration(b0 + 2 * t, "plain")
        iteration(b0 + 2 * t + 1, "plain")
        return carry

    lax.fori_loop(0, n_plain // 2, pair, 0)

    @pl.when(n_plain % 2 == 1)
    def _():
        iteration(b0 + n_plain - 1, "plain")

    @pl.when(n_full > b0)
    def _():
        iteration(n_full - 1, "masked")

    iteration(n_full, None)
    values(n_full)


def _normalised(acc_ref):
    return acc_ref[0:HEAD_W, :] * (1.0 / acc_ref[HEAD_W:HEAD_W + 1, :])


def _first_needed_block(h, i, slope, qmax_ref, kmax_ref, pmin_ref, pmax_ref):
    nblk = SEQ // TK
    qm = qmax_ref[h * nblk + i]
    row_max_floor = -(qm * kmax_ref[h * nblk + i])

    def skippable(b):
        gap = jnp.maximum(jnp.maximum(pmin_ref[i] - pmax_ref[b], pmin_ref[b] - pmax_ref[i]), 0.0)
        return qm * kmax_ref[h * nblk + b] - slope * gap - row_max_floor < -ALIBI_SKIP_LOG2

    return lax.while_loop(lambda b: jnp.logical_and(b < i, skippable(b)), lambda b: b + 1, jnp.int32(0))


def _attn_a_kernel(slope_ref, qmax_ref, kmax_ref, pmin_ref, pmax_ref, q_ref, k_ref, vt_ref, pq_ref, pk_ref,
                   dl_ref, g_ref, o_ref, s1, s2, p1, p2, acc1, acc2, st, *, lam_init):
    h = pl.program_id(0)
    i = pl.program_id(1)
    q = q_ref[...]
    lane = lax.broadcasted_iota(jnp.int32, (TQ_A, HEAD_W), 1)
    zero = jnp.zeros_like(q)
    q1 = jnp.where(lane < A_DHALF, q, zero)
    q2 = jnp.where(lane >= A_DHALF, q, zero)
    slope = slope_ref[h]
    aq = pq_ref[...] * slope

    def bias_fn(off):
        bk = pk_ref[pl.ds(off, TK), :] * slope
        d = aq - jnp.concatenate([bk] * (TQ_A // LANES), axis=1)
        return pltpu.bitcast(pltpu.bitcast(d, jnp.uint32) | jnp.uint32(0x80000000), F32)

    b0 = _first_needed_block(h, i, slope, qmax_ref, kmax_ref, pmin_ref, pmax_ref)
    _online_softmax(i, TQ_A, (k_ref, k_ref), (vt_ref, vt_ref), (q1, q2), (s1, s2), (p1, p2),
                    (acc1, acc2), st, bias_fn, b0)

    dl = dl_ref[...]
    lam = (jnp.exp(jnp.sum(dl[0:1] * dl[1:2], axis=1, keepdims=True))
           - jnp.exp(jnp.sum(dl[2:3] * dl[3:4], axis=1, keepdims=True)) + lam_init)
    o = _normalised(acc1) - lam * _normalised(acc2)
    r = lax.rsqrt(jnp.mean(o * o, axis=0, keepdims=True) + RMS_EPS)
    g = jnp.concatenate([g_ref[...]] * (TQ_A // LANES), axis=1)
    o_ref[...] = (o * r * g * (1.0 - lam_init)).T.astype(BF16)


def _attn_a(slopes, qmax, kmax, pmin, pmax, qk, vt, pq, pk, dl, g_b, lam_init):
    s = qk.shape[0]
    assert TQ_A == TK
    smem = pl.BlockSpec(memory_space=pltpu.SMEM)
    return pl.pallas_call(
        functools.partial(_attn_a_kernel, lam_init=lam_init),
        out_shape=jax.ShapeDtypeStruct((s, A_WIDTH), BF16),
        grid=(N_HEADS, s // TQ_A),
        in_specs=[smem, smem, smem, smem, smem,
                  pl.BlockSpec((TQ_A, HEAD_W), lambda h, i: (i, h)),
                  pl.BlockSpec((s, HEAD_W), lambda h, i: (0, N_HEADS + h)),
                  pl.BlockSpec((s // TK, VT_ROWS, TK), lambda h, i: (0, h, 0)),
                  pl.BlockSpec((1, TQ_A), lambda h, i: (0, i)),
                  pl.BlockSpec((s, LANES), lambda h, i: (0, 0)),
                  pl.BlockSpec((4, A_DHALF), lambda h, i: (0, 0)),
                  pl.BlockSpec((HEAD_W, LANES), lambda h, i: (0, 0))],
        out_specs=pl.BlockSpec((TQ_A, HEAD_W), lambda h, i: (i, h)),
        scratch_shapes=[pltpu.VMEM((TK, TQ_A), F32), pltpu.VMEM((TK, TQ_A), F32),
                        pltpu.VMEM((TK, TQ_A), BF16), pltpu.VMEM((TK, TQ_A), BF16),
                        pltpu.VMEM((VT_ROWS, TQ_A), F32), pltpu.VMEM((VT_ROWS, TQ_A), F32),
                        pltpu.VMEM((2 * ST_ROWS, TQ_A), F32)],
        compiler_params=_cparams(("parallel", "parallel")),
        name="attn_a",
    )(slopes, qmax, kmax, pmin, pmax, qk, qk, vt, pq, pk, dl, g_b)


def _attn_c_kernel(qa_ref, qb_ref, ka_ref, kb_ref, vta_ref, vtb_ref, o_ref, sa, sb, pa, pb, acca, accb, st):
    i = pl.program_id(1)
    _online_softmax(i, TQ_C, (ka_ref, kb_ref), (vta_ref, vtb_ref), (qa_ref[...], qb_ref[...]),
                    (sa, sb), (pa, pb), (acca, accb), st, None)
    for x, acc in enumerate((acca, accb)):
        o_ref[:, x * HEAD_W:(x + 1) * HEAD_W] = _normalised(acc).T.astype(BF16)


def _attn_c(q, k, vt):
    s = q.shape[0]

    def head(x):
        return (pl.BlockSpec((TQ_C, HEAD_W), lambda h, i: (i, 2 * h + x)),
                pl.BlockSpec((s, HEAD_W), lambda h, i: (0, 2 * h + x)),
                pl.BlockSpec((s // TK, VT_ROWS, TK), lambda h, i: (0, 2 * h + x, 0)))

    (qa, ka, va), (qb, kb, vb) = head(0), head(1)
    return pl.pallas_call(
        _attn_c_kernel,
        out_shape=jax.ShapeDtypeStruct((s, C_WIDTH), BF16),
        grid=(N_HEADS // 2, s // TQ_C),
        in_specs=[qa, qb, ka, kb, va, vb],
        out_specs=pl.BlockSpec((TQ_C, 2 * HEAD_W), lambda h, i: (i, h)),
        scratch_shapes=[pltpu.VMEM((TK, TQ_C), F32), pltpu.VMEM((TK, TQ_C), F32),
                        pltpu.VMEM((TK, TQ_C), BF16), pltpu.VMEM((TK, TQ_C), BF16),
                        pltpu.VMEM((VT_ROWS, TQ_C), F32), pltpu.VMEM((VT_ROWS, TQ_C), F32),
                        pltpu.VMEM((2 * ST_ROWS, TQ_C), F32)],
        compiler_params=_cparams(("parallel", "parallel")),
        name="attn_c",
    )(q, q, k, k, vt, vt)


def _out_ln_kernel(oa_ref, ob_ref, oc_ref, wa_ref, wb_ref, wc_ref, h_ref, g_ref, b_ref,
                   y_ref, yb_ref):
    m = _dot(oa_ref[...], wa_ref[...]) + _dot(ob_ref[...], wb_ref[...]) + _dot(oc_ref[...], wc_ref[...])
    y = _layer_norm(ALPHA * h_ref[...] + m, g_ref[...], b_ref[...])
    y_ref[...] = y
    yb_ref[...] = y.astype(BF16)


def _out_ln(oa, ob, oc, wa, wb, wc, h, g, b):
    s, d = h.shape
    row = pl.BlockSpec((TS, d), lambda i: (i, 0))
    vec = pl.BlockSpec((1, d), lambda i: (0, 0))

    def rows(a):
        return pl.BlockSpec((TS, a.shape[1]), lambda i: (i, 0))

    def full(a):
        return pl.BlockSpec(a.shape, lambda i: (0, 0))

    return pl.pallas_call(
        _out_ln_kernel,
        out_shape=(jax.ShapeDtypeStruct((s, d), F32), jax.ShapeDtypeStruct((s, d), BF16)),
        grid=(s // TS,),
        in_specs=[rows(oa), rows(ob), rows(oc), full(wa), full(wb), full(wc), row, vec, vec],
        out_specs=(row, row),
        compiler_params=_cparams(("parallel",)),
        name="out_ln",
    )(oa, ob, oc, wa, wb, wc, h, g.reshape(1, d), b.reshape(1, d))


def _ffn_kernel(xb_ref, wg_ref, wu_ref, cw_ref, wd_ref, h_ref, g_ref, b_ref, y_ref, yb_ref,
                gbuf, carry, acc):
    i = pl.program_id(0)
    j = pl.program_id(1)
    x = xb_ref[...]

    @pl.when(i == 0)
    def _():
        carry[j] = jnp.zeros((HALO, TF), F32)

    gbuf[pl.ds(0, HALO), :] = carry[j]
    gbuf[pl.ds(HALO, TS), :] = _dot(x, wg_ref[...])
    carry[j] = gbuf[pl.ds(TS, HALO), :]
    gate = _causal_conv3(gbuf, cw_ref, TS)
    act = (gate * (1.0 / (1.0 + jnp.exp(-gate))) * _dot(x, wu_ref[...])).astype(BF16)
    part = _dot(act, wd_ref[...])

    @pl.when(j == 0)
    def _():
        acc[...] = part

    @pl.when(j > 0)
    def _():
        acc[...] += part

    @pl.when(j == pl.num_programs(1) - 1)
    def _():
        y = _layer_norm(ALPHA * h_ref[...] + acc[...], g_ref[...], b_ref[...])
        y_ref[...] = y
        yb_ref[...] = y.astype(BF16)


def _ffn(xb, wg, wu, conv_w, wd, h, g, b):
    s, d = h.shape
    nf = D_FF // TF
    row = pl.BlockSpec((TS, d), lambda i, j: (i, 0))
    vec = pl.BlockSpec((1, d), lambda i, j: (0, 0))
    return pl.pallas_call(
        _ffn_kernel,
        out_shape=(jax.ShapeDtypeStruct((s, d), F32), jax.ShapeDtypeStruct((s, d), BF16)),
        grid=(s // TS, nf),
        in_specs=[row,
                  pl.BlockSpec((d, TF), lambda i, j: (0, j)),
                  pl.BlockSpec((d, TF), lambda i, j: (0, j)),
                  pl.BlockSpec((3, TF), lambda i, j: (0, j)),
                  pl.BlockSpec((TF, d), lambda i, j: (j, 0)),
                  row, vec, vec],
        out_specs=(row, row),
        scratch_shapes=[pltpu.VMEM((HALO + TS, TF), F32),
                        pltpu.VMEM((nf, HALO, TF), F32),
                        pltpu.VMEM((TS, d), F32)],
        compiler_params=_cparams(("arbitrary", "arbitrary")),
        name="ffn",
    )(xb, wg, wu, conv_w, wd, h, g.reshape(1, d), b.reshape(1, d))


def _pad_heads(w, width):
    kdim = w.shape[0]
    w = w.reshape(kdim, N_HEADS, width)
    return jnp.pad(w, ((0, 0), (0, 0), (0, HEAD_W - width))).reshape(kdim, N_HEADS * HEAD_W)


def _prep_layer(w_in, w_uq, w_ukv, w_o):
    aq, ak, av = 0, A_WIDTH, 2 * A_WIDTH
    bh = 3 * A_WIDTH
    cq = bh + 3 * B_WIDTH
    ckv = cq + C_Q_RANK
    ckr = ckv + C_KV_RANK
    kr_tile = jnp.pad(w_in[:, ckr:], ((0, 0), (C_NOPE, HEAD_W - C_NOPE - C_ROPE)))
    ukv = w_ukv.reshape(C_KV_RANK, N_HEADS, C_NOPE + HEAD_W)
    return dict(
        w_qk=w_in[:, aq:av].astype(BF16),
        w_vt=w_in[:, av:bh].T.astype(BF16),
        w_b=w_in[:, bh:cq].astype(BF16),
        w_c=jnp.concatenate([w_in[:, cq:ckr], kr_tile], axis=1).astype(BF16),
        w_uq=_pad_heads(w_uq, C_NOPE + C_ROPE).astype(BF16),
        w_uk=_pad_heads(ukv[:, :, :C_NOPE].reshape(C_KV_RANK, N_HEADS * C_NOPE), C_NOPE).astype(BF16),
        w_uvt=ukv[:, :, C_NOPE:].reshape(C_KV_RANK, C_WIDTH).T.astype(BF16),
        w_oa=w_o[:A_WIDTH].astype(BF16),
        w_ob=w_o[A_WIDTH:A_WIDTH + B_WIDTH].astype(BF16),
        w_oc=w_o[A_WIDTH + B_WIDTH:].astype(BF16),
    )


def kernel(x, positions, ln_in_g, ln_in_b, w_in, diff_lambda, diff_norm_g, conv_w, mla_q_norm_g, mla_kv_norm_g, w_uq, w_ukv, w_o, ln1_g, ln1_b, ffn_w_gate, ffn_w_up, ffn_conv_w, ffn_w_down, ln2_g, ln2_b):
    batch, s, d = x.shape
    assert (batch, s, d) == (1, SEQ, D_MODEL)
    pos = positions.reshape(s)
    pos_f = pos.astype(F32)
    pq = pos_f.reshape(1, s)
    pk = jnp.broadcast_to(pos_f.reshape(s, 1), (s, LANES))
    n_blk = s // TK
    pmin, pmax = pos_f.reshape(n_blk, TK).min(axis=1), pos_f.reshape(n_blk, TK).max(axis=1)
    heads = jnp.arange(N_HEADS, dtype=F32)
    slopes = (2.0 ** (-8.0 * (heads + 1.0) / N_HEADS)) * LOG2E
    qk_scale = jnp.concatenate([jnp.full((1, A_WIDTH), A_DHALF ** -0.5 * LOG2E, F32),
                                jnp.ones((1, A_WIDTH), F32)], axis=1)
    c_scale = (C_NOPE + C_ROPE) ** -0.5 * LOG2E

    cos, sina, sinb = _rope_tables(pos)
    h, hb = _ln_in(x.reshape(s, d), ln_in_g, ln_in_b)
    for l in range(DEPTH):
        lam_init = 0.8 - 0.6 * math.exp(-0.3 * l)
        w = _prep_layer(w_in[l], w_uq[l], w_ukv[l], w_o[l])
        qk, sq_norms = _proj_qk(hb, w["w_qk"], qk_scale)
        norms = jnp.sqrt(sq_norms[:, :, 0, :2 * N_HEADS]).reshape(2, n_blk, N_HEADS, 2).max(-1) * NORM_SLACK
        qmax, kmax = norms[0].T.reshape(-1), norms[1].T.reshape(-1)
        vt_a = _proj_vt(hb, w["w_vt"])
        g_b = jnp.broadcast_to(diff_norm_g[l].reshape(HEAD_W, 1), (HEAD_W, LANES))
        o_a = _attn_a(slopes, qmax, kmax, pmin, pmax, qk, vt_a, pq, pk, diff_lambda[l], g_b, lam_init)
        o_b = _mix_b(hb, w["w_b"], conv_w[l])
        q_c, k_c, vt_c = _proj_c(hb, w["w_c"], mla_q_norm_g[l].reshape(1, -1), mla_kv_norm_g[l].reshape(1, -1),
                                 w["w_uq"], w["w_uk"], w["w_uvt"], cos, sina, sinb, c_scale)
        o_c = _attn_c(q_c, k_c, vt_c)
        h, hb = _out_ln(o_a, o_b, o_c, w["w_oa"], w["w_ob"], w["w_oc"], h, ln1_g[l], ln1_b[l])
        h, hb = _ffn(hb, ffn_w_gate[l].astype(BF16), ffn_w_up[l].astype(BF16), ffn_conv_w[l],
                     ffn_w_down[l].astype(BF16), h, ln2_g[l], ln2_b[l])
    return h.reshape(batch, s, d)
```

```python
import functools
import math

import jax
import jax.numpy as jnp
from jax import lax
from jax.experimental import pallas as pl
from jax.experimental.pallas import tpu as pltpu

F32 = jnp.float32
BF16 = jnp.bfloat16

D_MODEL = 2048
SEQ = 16384
DEPTH = 2
CHUNK = 64
N_HEADS = 6
HEAD_W = 128
BF16_ROWS = 16
VT_ROWS = HEAD_W + BF16_ROWS
A_DHALF = 64
A_WIDTH = N_HEADS * HEAD_W
B_WIDTH = 512
C_NOPE = 64
C_ROPE = 32
C_WIDTH = N_HEADS * HEAD_W
C_Q_RANK = 512
C_KV_RANK = 256
ROPE_THETA = 10000.0
D_FF = 5632
ALPHA = (2.0 * DEPTH) ** 0.25
LN_EPS = 1e-5
RMS_EPS = 1e-6
NEG = -1e30
LOG2E = math.log2(math.e)
ALIBI_SKIP_LOG2 = 192.0
NORM_SLACK = 1.02

LANES = 128
TS = 512
TQ_A = 512
TQ_C = 512
TK = 512
TF = 512
HALO = 8
VMEM_LIMIT = 56 * 1024 * 1024


def _cparams(sem):
    return pltpu.CompilerParams(dimension_semantics=sem, vmem_limit_bytes=VMEM_LIMIT)


def _layer_norm(x, g, b):
    mu = jnp.mean(x, axis=-1, keepdims=True)
    xc = x - mu
    var = jnp.mean(xc * xc, axis=-1, keepdims=True)
    return xc * lax.rsqrt(var + LN_EPS) * g + b


def _rms_norm(x, g):
    return x * lax.rsqrt(jnp.mean(x * x, axis=-1, keepdims=True) + RMS_EPS) * g


def _dot(a, b):
    return jnp.dot(a, b, preferred_element_type=F32)


def _dot_nt(a, b):
    return lax.dot_general(a, b, (((1,), (1,)), ((), ())), preferred_element_type=F32)


def _ln_in_kernel(x_ref, g_ref, b_ref, h_ref, hb_ref):
    h = _layer_norm(x_ref[...], g_ref[...], b_ref[...])
    h_ref[...] = h
    hb_ref[...] = h.astype(BF16)


def _ln_in(x, g, b):
    s, d = x.shape
    row = pl.BlockSpec((TS, d), lambda i: (i, 0))
    vec = pl.BlockSpec((1, d), lambda i: (0, 0))
    return pl.pallas_call(
        _ln_in_kernel,
        out_shape=(jax.ShapeDtypeStruct((s, d), F32), jax.ShapeDtypeStruct((s, d), BF16)),
        grid=(s // TS,),
        in_specs=[row, vec, vec],
        out_specs=(row, row),
        compiler_params=_cparams(("parallel",)),
        name="ln_in",
    )(x, g.reshape(1, d), b.reshape(1, d))


def _rope_tab_kernel(pos_ref, invf_ref, ma_ref, mb_ref, cos_ref, sina_ref, sinb_ref):
    ang = pos_ref[...].astype(F32) * invf_ref[...]
    sn = jnp.sin(ang)
    cos_ref[...] = jnp.cos(ang)
    sina_ref[...] = sn * ma_ref[...]
    sinb_ref[...] = sn * mb_ref[...]


def _rope_tables(positions):
    s = positions.shape[0]
    half = C_ROPE // 2
    inv_freq = ROPE_THETA ** (-jnp.arange(0, C_ROPE, 2, dtype=F32) / C_ROPE)
    zeros = jnp.zeros((LANES,), F32)
    invf = zeros.at[C_NOPE:C_NOPE + half].set(inv_freq).at[C_NOPE + half:C_NOPE + C_ROPE].set(inv_freq)
    mask_a = zeros.at[C_NOPE + half:C_NOPE + C_ROPE].set(1.0)
    mask_b = zeros.at[C_NOPE:C_NOPE + half].set(-1.0)
    tab = jax.ShapeDtypeStruct((s, LANES), F32)
    row = pl.BlockSpec((TS, LANES), lambda i: (i, 0))
    vec = pl.BlockSpec((1, LANES), lambda i: (0, 0))
    return pl.pallas_call(
        _rope_tab_kernel,
        out_shape=(tab, tab, tab),
        grid=(s // TS,),
        in_specs=[pl.BlockSpec((TS, 1), lambda i: (i, 0)), vec, vec, vec],
        out_specs=(row, row, row),
        compiler_params=_cparams(("parallel",)),
        name="rope_tables",
    )(positions.reshape(s, 1), invf.reshape(1, LANES), mask_a.reshape(1, LANES), mask_b.reshape(1, LANES))


def _rope(x, cos, sina, sinb):
    half = C_ROPE // 2
    return x * cos + pltpu.roll(x, half, 1) * sina + pltpu.roll(x, LANES - half, 1) * sinb


def _proj_qk_kernel(x_ref, w_ref, s_ref, e_ref, o_ref, n_ref):
    o = (_dot(x_ref[...], w_ref[...]) * s_ref[...]).astype(BF16)
    o_ref[...] = o
    of = o.astype(F32)
    seg = _dot((of * of).astype(BF16), e_ref[...])
    n_ref[...] = jnp.broadcast_to(jnp.max(seg, axis=0, keepdims=True), (8, LANES))


def _proj_qk(hb, w, col_scale):
    s, d = hb.shape
    n = w.shape[1]
    tn = A_WIDTH
    seg_of_col = jnp.arange(tn, dtype=jnp.int32) // A_DHALF
    e = (seg_of_col[:, None] == jnp.arange(LANES, dtype=jnp.int32)[None, :]).astype(BF16)
    return pl.pallas_call(
        _proj_qk_kernel,
        out_shape=(jax.ShapeDtypeStruct((s, n), BF16),
                   jax.ShapeDtypeStruct((n // tn, s // TS, 8, LANES), F32)),
        grid=(n // tn, s // TS),
        in_specs=[pl.BlockSpec((TS, d), lambda j, i: (i, 0)),
                  pl.BlockSpec((d, tn), lambda j, i: (0, j)),
                  pl.BlockSpec((1, tn), lambda j, i: (0, j)),
                  pl.BlockSpec((tn, LANES), lambda j, i: (0, 0))],
        out_specs=(pl.BlockSpec((TS, tn), lambda j, i: (i, j)),
                   pl.BlockSpec((None, None, 8, LANES), lambda j, i: (j, i, 0, 0))),
        compiler_params=_cparams(("parallel", "parallel")),
        name="proj_qk",
    )(hb, w, col_scale, e)


def _store_vt(vt_ref, vt):
    ones_row = (lax.broadcasted_iota(jnp.int32, (BF16_ROWS, TK), 0) == 0).astype(BF16)
    for h in range(N_HEADS):
        vt_ref[h * VT_ROWS:h * VT_ROWS + HEAD_W, :] = vt[h * HEAD_W:(h + 1) * HEAD_W, :].astype(BF16)
        vt_ref[h * VT_ROWS + HEAD_W:(h + 1) * VT_ROWS, :] = ones_row


def _proj_vt_kernel(x_ref, w_ref, o_ref):
    _store_vt(o_ref, _dot_nt(w_ref[...], x_ref[...]))


def _proj_vt(hb, w_t):
    s, d = hb.shape
    n = w_t.shape[0]
    return pl.pallas_call(
        _proj_vt_kernel,
        out_shape=jax.ShapeDtypeStruct((s // TK, N_HEADS * VT_ROWS, TK), BF16),
        grid=(s // TK,),
        in_specs=[pl.BlockSpec((TK, d), lambda i: (i, 0)),
                  pl.BlockSpec((n, d), lambda i: (0, 0))],
        out_specs=pl.BlockSpec((None, N_HEADS * VT_ROWS, TK), lambda i: (i, 0, 0)),
        compiler_params=_cparams(("parallel",)),
        name="proj_vt",
    )(hb, w_t)


def _causal_conv3(buf_ref, cw_ref, rows):
    return (cw_ref[0:1, :] * buf_ref[pl.ds(HALO - 2, rows), :]
            + cw_ref[1:2, :] * buf_ref[pl.ds(HALO - 1, rows), :]
            + cw_ref[2:3, :] * buf_ref[pl.ds(HALO, rows), :])


def _mix_b_kernel(x_ref, w_ref, cw_ref, o_ref, ubuf):
    i = pl.program_id(0)

    @pl.when(i == 0)
    def _():
        ubuf[pl.ds(0, HALO), :] = jnp.zeros((HALO, B_WIDTH), F32)

    p = _dot(x_ref[...], w_ref[...])
    ubuf[pl.ds(HALO, TS), :] = p[:, 2 * B_WIDTH:] * p[:, :B_WIDTH]
    y = _causal_conv3(ubuf, cw_ref, TS)
    o_ref[...] = (p[:, B_WIDTH:2 * B_WIDTH] * y).astype(BF16)
    ubuf[pl.ds(0, HALO), :] = ubuf[pl.ds(TS, HALO), :]


def _mix_b(hb, w, conv_w):
    s, d = hb.shape
    return pl.pallas_call(
        _mix_b_kernel,
        out_shape=jax.ShapeDtypeStruct((s, B_WIDTH), BF16),
        grid=(s // TS,),
        in_specs=[pl.BlockSpec((TS, d), lambda i: (i, 0)),
                  pl.BlockSpec((d, 3 * B_WIDTH), lambda i: (0, 0)),
                  pl.BlockSpec((3, B_WIDTH), lambda i: (0, 0))],
        out_specs=pl.BlockSpec((TS, B_WIDTH), lambda i: (i, 0)),
        scratch_shapes=[pltpu.VMEM((HALO + TS, B_WIDTH), F32)],
        compiler_params=_cparams(("arbitrary",)),
        name="mix_b",
    )(hb, w, conv_w)


def _proj_c_kernel(x_ref, wc_ref, gq_ref, gkv_ref, wuq_ref, wuk_ref, wuvt_ref,
                   cos_ref, sina_ref, sinb_ref, q_ref, k_ref, vt_ref, *, q_scale):
    p = _dot(x_ref[...], wc_ref[...])
    cos, sina, sinb = cos_ref[...], sina_ref[...], sinb_ref[...]
    cq = _rms_norm(p[:, :C_Q_RANK], gq_ref[...]).astype(BF16)
    ckv = _rms_norm(p[:, C_Q_RANK:C_Q_RANK + C_KV_RANK], gkv_ref[...]).astype(BF16)
    k_pe = _rope(p[:, C_Q_RANK + C_KV_RANK:], cos, sina, sinb)
    qc = _dot(cq, wuq_ref[...])
    kc = _dot(ckv, wuk_ref[...])
    for h in range(N_HEADS):
        sl = slice(h * HEAD_W, (h + 1) * HEAD_W)
        q_ref[:, sl] = (_rope(qc[:, sl], cos, sina, sinb) * q_scale).astype(BF16)
        k_ref[:, sl] = (kc[:, sl] + k_pe).astype(BF16)
    _store_vt(vt_ref, _dot_nt(wuvt_ref[...], ckv))


def _proj_c(hb, wc, gq, gkv, wuq, wuk, wuvt, cos, sina, sinb, q_scale):
    s, d = hb.shape
    nc = wc.shape[1]
    row = pl.BlockSpec((TK, C_WIDTH), lambda i: (i, 0))
    tab = pl.BlockSpec((TK, LANES), lambda i: (i, 0))

    def full(a):
        return pl.BlockSpec(a.shape, lambda i: (0,) * a.ndim)

    return pl.pallas_call(
        functools.partial(_proj_c_kernel, q_scale=q_scale),
        out_shape=(jax.ShapeDtypeStruct((s, C_WIDTH), BF16),
                   jax.ShapeDtypeStruct((s, C_WIDTH), BF16),
                   jax.ShapeDtypeStruct((s // TK, N_HEADS * VT_ROWS, TK), BF16)),
        grid=(s // TK,),
        in_specs=[pl.BlockSpec((TK, d), lambda i: (i, 0)), full(wc), full(gq), full(gkv),
                  full(wuq), full(wuk), full(wuvt), tab, tab, tab],
        out_specs=(row, row, pl.BlockSpec((None, N_HEADS * VT_ROWS, TK), lambda i: (i, 0, 0))),
        compiler_params=_cparams(("parallel",)),
        name="proj_c",
    )(hb, wc, gq, gkv, wuq, wuk, wuvt, cos, sina, sinb)


ST_M, ST_ALPHA, ST_BM, ST_ROWS = 0, 1, 2, 4


def _stat(st_ref, x, row):
    r = x * ST_ROWS + row
    return st_ref.at[r:r + 1, :]


def _online_softmax(i, tq, k_refs, vt_refs, qs, s_refs, p_refs, acc_refs, st_ref, bias_fn, b0=0):
    n_maps = len(qs)
    n_full = (i * tq) // TK

    def scores(b, masked):
        off = pl.multiple_of(b * TK, TK)
        ks = {id(r): r[pl.ds(off, TK), :] for r in k_refs}
        bias = None if bias_fn is None else bias_fn(off)
        if masked:
            kidx = b * TK + lax.broadcasted_iota(jnp.int32, (TK, tq), 0)
            qidx = i * tq + lax.broadcasted_iota(jnp.int32, (TK, tq), 1)
            mask = (kidx // CHUNK) <= (qidx // CHUNK)
            if bias is not None:
                bias = jnp.where(mask, bias, NEG)
        for x in range(n_maps):
            s = _dot_nt(ks[id(k_refs[x])], qs[x])
            if bias is not None:
                s = s + bias
            elif masked:
                s = jnp.where(mask, s, NEG)
            s_refs[x][...] = s
            _stat(st_ref, x, ST_BM)[...] = jnp.max(s, axis=0, keepdims=True)

    def numerators(x):
        m_old = _stat(st_ref, x, ST_M)[...]
        m_new = jnp.maximum(m_old, _stat(st_ref, x, ST_BM)[...])
        _stat(st_ref, x, ST_M)[...] = m_new
        _stat(st_ref, x, ST_ALPHA)[...] = jnp.exp2(m_old - m_new)
        p_refs[x][...] = jnp.exp2(s_refs[x][...] - m_new).astype(BF16)

    def values(b):
        vts = {id(r): r[jnp.maximum(b, 0)] for r in vt_refs}
        for x in range(n_maps):
            pv = _dot(vts[id(vt_refs[x])], p_refs[x][...])
            acc_refs[x][...] = acc_refs[x][...] * _stat(st_ref, x, ST_ALPHA)[...] + pv

    def iteration(b, next_scores):
        values(b - 1)
        for x in range(n_maps):
            numerators(x)
        if next_scores is not None:
            scores(b + 1, masked=next_scores == "masked")

    for x in range(n_maps):
        p_refs[x][...] = jnp.zeros_like(p_refs[x])
        acc_refs[x][...] = jnp.zeros_like(acc_refs[x])
        _stat(st_ref, x, ST_M)[...] = jnp.full((1, tq), NEG, F32)
        _stat(st_ref, x, ST_ALPHA)[...] = jnp.ones((1, tq), F32)
    scores(b0, masked=True)

    n_plain = jnp.maximum(n_full - b0 - 1, 0)

    def pair(t, carry):
        iteration(b0 + 2 * t, "plain")
        iteration(b0 + 2 * t + 1, "plain")
        return carry

    lax.fori_loop(0, n_plain // 2, pair, 0)

    @pl.when(n_plain % 2 == 1)
    def _():
        iteration(b0 + n_plain - 1, "plain")

    @pl.when(n_full > b0)
    def _():
        iteration(n_full - 1, "masked")

    iteration(n_full, None)
    values(n_full)


def _normalised(acc_ref):
    return acc_ref[0:HEAD_W, :] * (1.0 / acc_ref[HEAD_W:HEAD_W + 1, :])


def _first_needed_block(h, i, slope, qmax_ref, kmax_ref, pmin_ref, pmax_ref):
    nblk = SEQ // TK
    qm = qmax_ref[h * nblk + i]
    row_max_floor = -(qm * kmax_ref[h * nblk + i])

    def skippable(b):
        gap = jnp.maximum(jnp.maximum(pmin_ref[i] - pmax_ref[b], pmin_ref[b] - pmax_ref[i]), 0.0)
        return qm * kmax_ref[h * nblk + b] - slope * gap - row_max_floor < -ALIBI_SKIP_LOG2

    return lax.while_loop(lambda b: jnp.logical_and(b < i, skippable(b)), lambda b: b + 1, jnp.int32(0))


def _attn_a_kernel(slope_ref, qmax_ref, kmax_ref, pmin_ref, pmax_ref, q_ref, k_ref, vt_ref, pq_ref, pk_ref,
                   dl_ref, g_ref, o_ref, s1, s2, p1, p2, acc1, acc2, st, *, lam_init):
    h = pl.program_id(0)
    i = pl.program_id(1)
    q = q_ref[...]
    lane = lax.broadcasted_iota(jnp.int32, (TQ_A, HEAD_W), 1)
    zero = jnp.zeros_like(q)
    q1 = jnp.where(lane < A_DHALF, q, zero)
    q2 = jnp.where(lane >= A_DHALF, q, zero)
    slope = slope_ref[h]
    aq = pq_ref[...] * slope

    def bias_fn(off):
        bk = pk_ref[pl.ds(off, TK), :] * slope
        d = aq - jnp.concatenate([bk] * (TQ_A // LANES), axis=1)
        return pltpu.bitcast(pltpu.bitcast(d, jnp.uint32) | jnp.uint32(0x80000000), F32)

    b0 = _first_needed_block(h, i, slope, qmax_ref, kmax_ref, pmin_ref, pmax_ref)
    _online_softmax(i, TQ_A, (k_ref, k_ref), (vt_ref, vt_ref), (q1, q2), (s1, s2), (p1, p2),
                    (acc1, acc2), st, bias_fn, b0)

    dl = dl_ref[...]
    lam = (jnp.exp(jnp.sum(dl[0:1] * dl[1:2], axis=1, keepdims=True))
           - jnp.exp(jnp.sum(dl[2:3] * dl[3:4], axis=1, keepdims=True)) + lam_init)
    o = _normalised(acc1) - lam * _normalised(acc2)
    r = lax.rsqrt(jnp.mean(o * o, axis=0, keepdims=True) + RMS_EPS)
    g = jnp.concatenate([g_ref[...]] * (TQ_A // LANES), axis=1)
    o_ref[...] = (o * r * g * (1.0 - lam_init)).T.astype(BF16)


def _attn_a(slopes, qmax, kmax, pmin, pmax, qk, vt, pq, pk, dl, g_b, lam_init):
    s = qk.shape[0]
    assert TQ_A == TK
    smem = pl.BlockSpec(memory_space=pltpu.SMEM)
    return pl.pallas_call(
        functools.partial(_attn_a_kernel, lam_init=lam_init),
        out_shape=jax.ShapeDtypeStruct((s, A_WIDTH), BF16),
        grid=(N_HEADS, s // TQ_A),
        in_specs=[smem, smem, smem, smem, smem,
                  pl.BlockSpec((TQ_A, HEAD_W), lambda h, i: (i, h)),
                  pl.BlockSpec((s, HEAD_W), lambda h, i: (0, N_HEADS + h)),
                  pl.BlockSpec((s // TK, VT_ROWS, TK), lambda h, i: (0, h, 0)),
                  pl.BlockSpec((1, TQ_A), lambda h, i: (0, i)),
                  pl.BlockSpec((s, LANES), lambda h, i: (0, 0)),
                  pl.BlockSpec((4, A_DHALF), lambda h, i: (0, 0)),
                  pl.BlockSpec((HEAD_W, LANES), lambda h, i: (0, 0))],
        out_specs=pl.BlockSpec((TQ_A, HEAD_W), lambda h, i: (i, h)),
        scratch_shapes=[pltpu.VMEM((TK, TQ_A), F32), pltpu.VMEM((TK, TQ_A), F32),
                        pltpu.VMEM((TK, TQ_A), BF16), pltpu.VMEM((TK, TQ_A), BF16),
                        pltpu.VMEM((VT_ROWS, TQ_A), F32), pltpu.VMEM((VT_ROWS, TQ_A), F32),
                        pltpu.VMEM((2 * ST_ROWS, TQ_A), F32)],
        compiler_params=_cparams(("parallel", "parallel")),
        name="attn_a",
    )(slopes, qmax, kmax, pmin, pmax, qk, qk, vt, pq, pk, dl, g_b)


def _attn_c_kernel(qa_ref, qb_ref, ka_ref, kb_ref, vta_ref, vtb_ref, o_ref, sa, sb, pa, pb, acca, accb, st):
    i = pl.program_id(1)
    _online_softmax(i, TQ_C, (ka_ref, kb_ref), (vta_ref, vtb_ref), (qa_ref[...], qb_ref[...]),
                    (sa, sb), (pa, pb), (acca, accb), st, None)
    for x, acc in enumerate((acca, accb)):
        o_ref[:, x * HEAD_W:(x + 1) * HEAD_W] = _normalised(acc).T.astype(BF16)


def _attn_c(q, k, vt):
    s = q.shape[0]

    def head(x):
        return (pl.BlockSpec((TQ_C, HEAD_W), lambda h, i: (i, 2 * h + x)),
                pl.BlockSpec((s, HEAD_W), lambda h, i: (0, 2 * h + x)),
                pl.BlockSpec((s // TK, VT_ROWS, TK), lambda h, i: (0, 2 * h + x, 0)))

    (qa, ka, va), (qb, kb, vb) = head(0), head(1)
    return pl.pallas_call(
        _attn_c_kernel,
        out_shape=jax.ShapeDtypeStruct((s, C_WIDTH), BF16),
        grid=(N_HEADS // 2, s // TQ_C),
        in_specs=[qa, qb, ka, kb, va, vb],
        out_specs=pl.BlockSpec((TQ_C, 2 * HEAD_W), lambda h, i: (i, h)),
        scratch_shapes=[pltpu.VMEM((TK, TQ_C), F32), pltpu.VMEM((TK, TQ_C), F32),
                        pltpu.VMEM((TK, TQ_C), BF16), pltpu.VMEM((TK, TQ_C), BF16),
                        pltpu.VMEM((VT_ROWS, TQ_C), F32), pltpu.VMEM((VT_ROWS, TQ_C), F32),
                        pltpu.VMEM((2 * ST_ROWS, TQ_C), F32)],
        compiler_params=_cparams(("parallel", "parallel")),
        name="attn_c",
    )(q, q, k, k, vt, vt)


def _out_ln_kernel(oa_ref, ob_ref, oc_ref, wa_ref, wb_ref, wc_ref, h_ref, g_ref, b_ref,
                   y_ref, yb_ref):
    m = _dot(oa_ref[...], wa_ref[...]) + _dot(ob_ref[...], wb_ref[...]) + _dot(oc_ref[...], wc_ref[...])
    y = _layer_norm(ALPHA * h_ref[...] + m, g_ref[...], b_ref[...])
    y_ref[...] = y
    yb_ref[...] = y.astype(BF16)


def _out_ln(oa, ob, oc, wa, wb, wc, h, g, b):
    s, d = h.shape
    row = pl.BlockSpec((TS, d), lambda i: (i, 0))
    vec = pl.BlockSpec((1, d), lambda i: (0, 0))

    def rows(a):
        return pl.BlockSpec((TS, a.shape[1]), lambda i: (i, 0))

    def full(a):
        return pl.BlockSpec(a.shape, lambda i: (0, 0))

    return pl.pallas_call(
        _out_ln_kernel,
        out_shape=(jax.ShapeDtypeStruct((s, d), F32), jax.ShapeDtypeStruct((s, d), BF16)),
        grid=(s // TS,),
        in_specs=[rows(oa), rows(ob), rows(oc), full(wa), full(wb), full(wc), row, vec, vec],
        out_specs=(row, row),
        compiler_params=_cparams(("parallel",)),
        name="out_ln",
    )(oa, ob, oc, wa, wb, wc, h, g.reshape(1, d), b.reshape(1, d))


def _ffn_kernel(xb_ref, wg_ref, wu_ref, cw_ref, wd_ref, h_ref, g_ref, b_ref, y_ref, yb_ref,
                gbuf, carry, acc):
    i = pl.program_id(0)
    j = pl.program_id(1)

    @pl.when(i == 0)
    def _():
        carry[j] = jnp.zeros((HALO, TF), F32)

    @pl.when(j == 0)
    def _():
        acc[...] = jnp.zeros_like(acc)

    gbuf[pl.ds(0, HALO), :] = carry[j]
    gbuf[pl.ds(HALO, TS), :] = _dot(xb_ref[...], wg_ref[...])
    carry[j] = gbuf[pl.ds(TS, HALO), :]
    gate = _causal_conv3(gbuf, cw_ref, TS)
    act = (gate * (1.0 / (1.0 + jnp.exp(-gate))) * _dot(xb_ref[...], wu_ref[...])).astype(BF16)
    acc[...] += _dot(act, wd_ref[...])

    @pl.when(j == pl.num_programs(1) - 1)
    def _():
        y = _layer_norm(ALPHA * h_ref[...] + acc[...], g_ref[...], b_ref[...])
        y_ref[...] = y
        yb_ref[...] = y.astype(BF16)


def _ffn(xb, wg, wu, conv_w, wd, h, g, b):
    s, d = h.shape
    nf = D_FF // TF
    row = pl.BlockSpec((TS, d), lambda i, j: (i, 0))
    vec = pl.BlockSpec((1, d), lambda i, j: (0, 0))
    return pl.pallas_call(
        _ffn_kernel,
        out_shape=(jax.ShapeDtypeStruct((s, d), F32), jax.ShapeDtypeStruct((s, d), BF16)),
        grid=(s // TS, nf),
        in_specs=[row,
                  pl.BlockSpec((d, TF), lambda i, j: (0, j)),
                  pl.BlockSpec((d, TF), lambda i, j: (0, j)),
                  pl.BlockSpec((3, TF), lambda i, j: (0, j)),
                  pl.BlockSpec((TF, d), lambda i, j: (j, 0)),
                  row, vec, vec],
        out_specs=(row, row),
        scratch_shapes=[pltpu.VMEM((HALO + TS, TF), F32),
                        pltpu.VMEM((nf, HALO, TF), F32),
                        pltpu.VMEM((TS, d), F32)],
        compiler_params=_cparams(("arbitrary", "arbitrary")),
        name="ffn",
    )(xb, wg, wu, conv_w, wd, h, g.reshape(1, d), b.reshape(1, d))


def _pad_heads(w, width):
    kdim = w.shape[0]
    w = w.reshape(kdim, N_HEADS, width)
    return jnp.pad(w, ((0, 0), (0, 0), (0, HEAD_W - width))).reshape(kdim, N_HEADS * HEAD_W)


def _prep_layer(w_in, w_uq, w_ukv, w_o):
    aq, ak, av = 0, A_WIDTH, 2 * A_WIDTH
    bh = 3 * A_WIDTH
    cq = bh + 3 * B_WIDTH
    ckv = cq + C_Q_RANK
    ckr = ckv + C_KV_RANK
    kr_tile = jnp.pad(w_in[:, ckr:], ((0, 0), (C_NOPE, HEAD_W - C_NOPE - C_ROPE)))
    ukv = w_ukv.reshape(C_KV_RANK, N_HEADS, C_NOPE + HEAD_W)
    return dict(
        w_qk=w_in[:, aq:av].astype(BF16),
        w_vt=w_in[:, av:bh].T.astype(BF16),
        w_b=w_in[:, bh:cq].astype(BF16),
        w_c=jnp.concatenate([w_in[:, cq:ckr], kr_tile], axis=1).astype(BF16),
        w_uq=_pad_heads(w_uq, C_NOPE + C_ROPE).astype(BF16),
        w_uk=_pad_heads(ukv[:, :, :C_NOPE].reshape(C_KV_RANK, N_HEADS * C_NOPE), C_NOPE).astype(BF16),
        w_uvt=ukv[:, :, C_NOPE:].reshape(C_KV_RANK, C_WIDTH).T.astype(BF16),
        w_oa=w_o[:A_WIDTH].astype(BF16),
        w_ob=w_o[A_WIDTH:A_WIDTH + B_WIDTH].astype(BF16),
        w_oc=w_o[A_WIDTH + B_WIDTH:].astype(BF16),
    )


def kernel(x, positions, ln_in_g, ln_in_b, w_in, diff_lambda, diff_norm_g, conv_w, mla_q_norm_g, mla_kv_norm_g, w_uq, w_ukv, w_o, ln1_g, ln1_b, ffn_w_gate, ffn_w_up, ffn_conv_w, ffn_w_down, ln2_g, ln2_b):
    batch, s, d = x.shape
    assert (batch, s, d) == (1, SEQ, D_MODEL)
    pos = positions.reshape(s)
    pos_f = pos.astype(F32)
    pq = pos_f.reshape(1, s)
    pk = jnp.broadcast_to(pos_f.reshape(s, 1), (s, LANES))
    n_blk = s // TK
    pmin, pmax = pos_f.reshape(n_blk, TK).min(axis=1), pos_f.reshape(n_blk, TK).max(axis=1)
    heads = jnp.arange(N_HEADS, dtype=F32)
    slopes = (2.0 ** (-8.0 * (heads + 1.0) / N_HEADS)) * LOG2E
    qk_scale = jnp.concatenate([jnp.full((1, A_WIDTH), A_DHALF ** -0.5 * LOG2E, F32),
                                jnp.ones((1, A_WIDTH), F32)], axis=1)
    c_scale = (C_NOPE + C_ROPE) ** -0.5 * LOG2E

    cos, sina, sinb = _rope_tables(pos)
    h, hb = _ln_in(x.reshape(s, d), ln_in_g, ln_in_b)
    for l in range(DEPTH):
        lam_init = 0.8 - 0.6 * math.exp(-0.3 * l)
        w = _prep_layer(w_in[l], w_uq[l], w_ukv[l], w_o[l])
        qk, sq_norms = _proj_qk(hb, w["w_qk"], qk_scale)
        norms = jnp.sqrt(sq_norms[:, :, 0, :2 * N_HEADS]).reshape(2, n_blk, N_HEADS, 2).max(-1) * NORM_SLACK
        qmax, kmax = norms[0].T.reshape(-1), norms[1].T.reshape(-1)
        vt_a = _proj_vt(hb, w["w_vt"])
        g_b = jnp.broadcast_to(diff_norm_g[l].reshape(HEAD_W, 1), (HEAD_W, LANES))
        o_a = _attn_a(slopes, qmax, kmax, pmin, pmax, qk, vt_a, pq, pk, diff_lambda[l], g_b, lam_init)
        o_b = _mix_b(hb, w["w_b"], conv_w[l])
        q_c, k_c, vt_c = _proj_c(hb, w["w_c"], mla_q_norm_g[l].reshape(1, -1), mla_kv_norm_g[l].reshape(1, -1),
                                 w["w_uq"], w["w_uk"], w["w_uvt"], cos, sina, sinb, c_scale)
        o_c = _attn_c(q_c, k_c, vt_c)
        h, hb = _out_ln(o_a, o_b, o_c, w["w_oa"], w["w_ob"], w["w_oc"], h, ln1_g[l], ln1_b[l])
        h, hb = _ffn(hb, ffn_w_gate[l].astype(BF16), ffn_w_up[l].astype(BF16), ffn_conv_w[l],
                     ffn_w_down[l].astype(BF16), h, ln2_g[l], ln2_b[l])
    return h.reshape(batch, s, d)
```

```python
import functools
import math

import jax
import jax.numpy as jnp
from jax import lax
from jax.experimental import pallas as pl
from jax.experimental.pallas import tpu as pltpu

F32 = jnp.float32
BF16 = jnp.bfloat16

D_MODEL = 2048
SEQ = 16384
DEPTH = 2
CHUNK = 64
N_HEADS = 6
HEAD_W = 128
BF16_ROWS = 16
VT_ROWS = HEAD_W + BF16_ROWS
A_DHALF = 64
A_WIDTH = N_HEADS * HEAD_W
B_WIDTH = 512
C_NOPE = 64
C_ROPE = 32
C_WIDTH = N_HEADS * HEAD_W
C_Q_RANK = 512
C_KV_RANK = 256
ROPE_THETA = 10000.0
D_FF = 5632
ALPHA = (2.0 * DEPTH) ** 0.25
LN_EPS = 1e-5
RMS_EPS = 1e-6
NEG = -1e30
LOG2E = math.log2(math.e)
ALIBI_SKIP_LOG2 = 192.0
NORM_SLACK = 1.02

LANES = 128
TS = 512
TQ_A = 512
TQ_C = 512
TK = 512
TF = 512
HALO = 8
VMEM_LIMIT = 56 * 1024 * 1024


def _cparams(sem):
    return pltpu.CompilerParams(dimension_semantics=sem, vmem_limit_bytes=VMEM_LIMIT)


def _layer_norm(x, g, b):
    mu = jnp.mean(x, axis=-1, keepdims=True)
    xc = x - mu
    var = jnp.mean(xc * xc, axis=-1, keepdims=True)
    return xc * lax.rsqrt(var + LN_EPS) * g + b


def _rms_norm(x, g):
    return x * lax.rsqrt(jnp.mean(x * x, axis=-1, keepdims=True) + RMS_EPS) * g


def _dot(a, b):
    return jnp.dot(a, b, preferred_element_type=F32)


def _dot_nt(a, b):
    return lax.dot_general(a, b, (((1,), (1,)), ((), ())), preferred_element_type=F32)


def _ln_in_kernel(x_ref, g_ref, b_ref, h_ref, hb_ref):
    h = _layer_norm(x_ref[...], g_ref[...], b_ref[...])
    h_ref[...] = h
    hb_ref[...] = h.astype(BF16)


def _ln_in(x, g, b):
    s, d = x.shape
    row = pl.BlockSpec((TS, d), lambda i: (i, 0))
    vec = pl.BlockSpec((1, d), lambda i: (0, 0))
    return pl.pallas_call(
        _ln_in_kernel,
        out_shape=(jax.ShapeDtypeStruct((s, d), F32), jax.ShapeDtypeStruct((s, d), BF16)),
        grid=(s // TS,),
        in_specs=[row, vec, vec],
        out_specs=(row, row),
        compiler_params=_cparams(("parallel",)),
        name="ln_in",
    )(x, g.reshape(1, d), b.reshape(1, d))


def _rope_tab_kernel(pos_ref, invf_ref, ma_ref, mb_ref, cos_ref, sina_ref, sinb_ref):
    ang = pos_ref[...].astype(F32) * invf_ref[...]
    sn = jnp.sin(ang)
    cos_ref[...] = jnp.cos(ang)
    sina_ref[...] = sn * ma_ref[...]
    sinb_ref[...] = sn * mb_ref[...]


def _rope_tables(positions):
    s = positions.shape[0]
    half = C_ROPE // 2
    inv_freq = ROPE_THETA ** (-jnp.arange(0, C_ROPE, 2, dtype=F32) / C_ROPE)
    zeros = jnp.zeros((LANES,), F32)
    invf = zeros.at[C_NOPE:C_NOPE + half].set(inv_freq).at[C_NOPE + half:C_NOPE + C_ROPE].set(inv_freq)
    mask_a = zeros.at[C_NOPE + half:C_NOPE + C_ROPE].set(1.0)
    mask_b = zeros.at[C_NOPE:C_NOPE + half].set(-1.0)
    tab = jax.ShapeDtypeStruct((s, LANES), F32)
    row = pl.BlockSpec((TS, LANES), lambda i: (i, 0))
    vec = pl.BlockSpec((1, LANES), lambda i: (0, 0))
    return pl.pallas_call(
        _rope_tab_kernel,
        out_shape=(tab, tab, tab),
        grid=(s // TS,),
        in_specs=[pl.BlockSpec((TS, 1), lambda i: (i, 0)), vec, vec, vec],
        out_specs=(row, row, row),
        compiler_params=_cparams(("parallel",)),
        name="rope_tables",
    )(positions.reshape(s, 1), invf.reshape(1, LANES), mask_a.reshape(1, LANES), mask_b.reshape(1, LANES))


def _rope(x, cos, sina, sinb):
    half = C_ROPE // 2
    return x * cos + pltpu.roll(x, half, 1) * sina + pltpu.roll(x, LANES - half, 1) * sinb


def _proj_qk_kernel(x_ref, w_ref, s_ref, e_ref, o_ref, n_ref):
    o = (_dot(x_ref[...], w_ref[...]) * s_ref[...]).astype(BF16)
    o_ref[...] = o
    of = o.astype(F32)
    seg = _dot((of * of).astype(BF16), e_ref[...])
    n_ref[...] = jnp.broadcast_to(jnp.max(seg, axis=0, keepdims=True), (8, LANES))


def _proj_qk(hb, w, col_scale):
    s, d = hb.shape
    n = w.shape[1]
    tn = A_WIDTH
    seg_of_col = jnp.arange(tn, dtype=jnp.int32) // A_DHALF
    e = (seg_of_col[:, None] == jnp.arange(LANES, dtype=jnp.int32)[None, :]).astype(BF16)
    return pl.pallas_call(
        _proj_qk_kernel,
        out_shape=(jax.ShapeDtypeStruct((s, n), BF16),
                   jax.ShapeDtypeStruct((n // tn, s // TS, 8, LANES), F32)),
        grid=(n // tn, s // TS),
        in_specs=[pl.BlockSpec((TS, d), lambda j, i: (i, 0)),
                  pl.BlockSpec((d, tn), lambda j, i: (0, j)),
                  pl.BlockSpec((1, tn), lambda j, i: (0, j)),
                  pl.BlockSpec((tn, LANES), lambda j, i: (0, 0))],
        out_specs=(pl.BlockSpec((TS, tn), lambda j, i: (i, j)),
                   pl.BlockSpec((None, None, 8, LANES), lambda j, i: (j, i, 0, 0))),
        compiler_params=_cparams(("parallel", "parallel")),
        name="proj_qk",
    )(hb, w, col_scale, e)


def _store_vt(vt_ref, vt):
    ones_row = (lax.broadcasted_iota(jnp.int32, (BF16_ROWS, TK), 0) == 0).astype(BF16)
    for h in range(N_HEADS):
        vt_ref[h * VT_ROWS:h * VT_ROWS + HEAD_W, :] = vt[h * HEAD_W:(h + 1) * HEAD_W, :].astype(BF16)
        vt_ref[h * VT_ROWS + HEAD_W:(h + 1) * VT_ROWS, :] = ones_row


def _proj_vt_kernel(x_ref, w_ref, o_ref):
    _store_vt(o_ref, _dot_nt(w_ref[...], x_ref[...]))


def _proj_vt(hb, w_t):
    s, d = hb.shape
    n = w_t.shape[0]
    return pl.pallas_call(
        _proj_vt_kernel,
        out_shape=jax.ShapeDtypeStruct((s // TK, N_HEADS * VT_ROWS, TK), BF16),
        grid=(s // TK,),
        in_specs=[pl.BlockSpec((TK, d), lambda i: (i, 0)),
                  pl.BlockSpec((n, d), lambda i: (0, 0))],
        out_specs=pl.BlockSpec((None, N_HEADS * VT_ROWS, TK), lambda i: (i, 0, 0)),
        compiler_params=_cparams(("parallel",)),
        name="proj_vt",
    )(hb, w_t)


def _causal_conv3(buf_ref, cw_ref, rows):
    return (cw_ref[0:1, :] * buf_ref[pl.ds(HALO - 2, rows), :]
            + cw_ref[1:2, :] * buf_ref[pl.ds(HALO - 1, rows), :]
            + cw_ref[2:3, :] * buf_ref[pl.ds(HALO, rows), :])


def _mix_b_kernel(x_ref, w_ref, cw_ref, o_ref, ubuf):
    i = pl.program_id(0)

    @pl.when(i == 0)
    def _():
        ubuf[pl.ds(0, HALO), :] = jnp.zeros((HALO, B_WIDTH), F32)

    p = _dot(x_ref[...], w_ref[...])
    ubuf[pl.ds(HALO, TS), :] = p[:, 2 * B_WIDTH:] * p[:, :B_WIDTH]
    y = _causal_conv3(ubuf, cw_ref, TS)
    o_ref[...] = (p[:, B_WIDTH:2 * B_WIDTH] * y).astype(BF16)
    ubuf[pl.ds(0, HALO), :] = ubuf[pl.ds(TS, HALO), :]


def _mix_b(hb, w, conv_w):
    s, d = hb.shape
    return pl.pallas_call(
        _mix_b_kernel,
        out_shape=jax.ShapeDtypeStruct((s, B_WIDTH), BF16),
        grid=(s // TS,),
        in_specs=[pl.BlockSpec((TS, d), lambda i: (i, 0)),
                  pl.BlockSpec((d, 3 * B_WIDTH), lambda i: (0, 0)),
                  pl.BlockSpec((3, B_WIDTH), lambda i: (0, 0))],
        out_specs=pl.BlockSpec((TS, B_WIDTH), lambda i: (i, 0)),
        scratch_shapes=[pltpu.VMEM((HALO + TS, B_WIDTH), F32)],
        compiler_params=_cparams(("arbitrary",)),
        name="mix_b",
    )(hb, w, conv_w)


def _proj_c_kernel(x_ref, wc_ref, gq_ref, gkv_ref, wuq_ref, wuk_ref, wuvt_ref,
                   cos_ref, sina_ref, sinb_ref, q_ref, k_ref, vt_ref, *, q_scale):
    p = _dot(x_ref[...], wc_ref[...])
    cos, sina, sinb = cos_ref[...], sina_ref[...], sinb_ref[...]
    cq = _rms_norm(p[:, :C_Q_RANK], gq_ref[...]).astype(BF16)
    ckv = _rms_norm(p[:, C_Q_RANK:C_Q_RANK + C_KV_RANK], gkv_ref[...]).astype(BF16)
    k_pe = _rope(p[:, C_Q_RANK + C_KV_RANK:], cos, sina, sinb)
    qc = _dot(cq, wuq_ref[...])
    kc = _dot(ckv, wuk_ref[...])
    for h in range(N_HEADS):
        sl = slice(h * HEAD_W, (h + 1) * HEAD_W)
        q_ref[:, sl] = (_rope(qc[:, sl], cos, sina, sinb) * q_scale).astype(BF16)
        k_ref[:, sl] = (kc[:, sl] + k_pe).astype(BF16)
    _store_vt(vt_ref, _dot_nt(wuvt_ref[...], ckv))


def _proj_c(hb, wc, gq, gkv, wuq, wuk, wuvt, cos, sina, sinb, q_scale):
    s, d = hb.shape
    nc = wc.shape[1]
    row = pl.BlockSpec((TK, C_WIDTH), lambda i: (i, 0))
    tab = pl.BlockSpec((TK, LANES), lambda i: (i, 0))

    def full(a):
        return pl.BlockSpec(a.shape, lambda i: (0,) * a.ndim)

    return pl.pallas_call(
        functools.partial(_proj_c_kernel, q_scale=q_scale),
        out_shape=(jax.ShapeDtypeStruct((s, C_WIDTH), BF16),
                   jax.ShapeDtypeStruct((s, C_WIDTH), BF16),
                   jax.ShapeDtypeStruct((s // TK, N_HEADS * VT_ROWS, TK), BF16)),
        grid=(s // TK,),
        in_specs=[pl.BlockSpec((TK, d), lambda i: (i, 0)), full(wc), full(gq), full(gkv),
                  full(wuq), full(wuk), full(wuvt), tab, tab, tab],
        out_specs=(row, row, pl.BlockSpec((None, N_HEADS * VT_ROWS, TK), lambda i: (i, 0, 0))),
        compiler_params=_cparams(("parallel",)),
        name="proj_c",
    )(hb, wc, gq, gkv, wuq, wuk, wuvt, cos, sina, sinb)


ST_M, ST_ALPHA, ST_BM, ST_ROWS = 0, 1, 2, 4


def _stat(st_ref, x, row):
    r = x * ST_ROWS + row
    return st_ref.at[r:r + 1, :]


def _online_softmax(i, tq, k_refs, vt_refs, qs, s_refs, p_refs, acc_refs, st_ref, bias_fn, b0=0):
    n_maps = len(qs)
    n_full = (i * tq) // TK

    def scores(b, masked):
        off = pl.multiple_of(b * TK, TK)
        ks = {id(r): r[pl.ds(off, TK), :] for r in k_refs}
        bias = None if bias_fn is None else bias_fn(off)
        if masked:
            kidx = b * TK + lax.broadcasted_iota(jnp.int32, (TK, tq), 0)
            qidx = i * tq + lax.broadcasted_iota(jnp.int32, (TK, tq), 1)
            mask = (kidx // CHUNK) <= (qidx // CHUNK)
            if bias is not None:
                bias = jnp.where(mask, bias, NEG)
        for x in range(n_maps):
            s = _dot_nt(ks[id(k_refs[x])], qs[x])
            if bias is not None:
                s = s + bias
            elif masked:
                s = jnp.where(mask, s, NEG)
            s_refs[x][...] = s
            _stat(st_ref, x, ST_BM)[...] = jnp.max(s, axis=0, keepdims=True)

    def numerators(x):
        m_old = _stat(st_ref, x, ST_M)[...]
        m_new = jnp.maximum(m_old, _stat(st_ref, x, ST_BM)[...])
        _stat(st_ref, x, ST_M)[...] = m_new
        _stat(st_ref, x, ST_ALPHA)[...] = jnp.exp2(m_old - m_new)
        p_refs[x][...] = jnp.exp2(s_refs[x][...] - m_new).astype(BF16)

    def values(b):
        vts = {id(r): r[jnp.maximum(b, 0)] for r in vt_refs}
        for x in range(n_maps):
            pv = _dot(vts[id(vt_refs[x])], p_refs[x][...])
            acc_refs[x][...] = acc_refs[x][...] * _stat(st_ref, x, ST_ALPHA)[...] + pv

    def iteration(b, next_scores):
        values(b - 1)
        for x in range(n_maps):
            numerators(x)
        if next_scores is not None:
            scores(b + 1, masked=next_scores == "masked")

    for x in range(n_maps):
        p_refs[x][...] = jnp.zeros_like(p_refs[x])
        acc_refs[x][...] = jnp.zeros_like(acc_refs[x])
        _stat(st_ref, x, ST_M)[...] = jnp.full((1, tq), NEG, F32)
        _stat(st_ref, x, ST_ALPHA)[...] = jnp.ones((1, tq), F32)
    scores(b0, masked=True)

    n_plain = jnp.maximum(n_full - b0 - 1, 0)

    def pair(t, carry):
        iteration(b0 + 2 * t, "plain")
        iteration(b0 + 2 * t + 1, "plain")
        return carry

    lax.fori_loop(0, n_plain // 2, pair, 0)

    @pl.when(n_plain % 2 == 1)
    def _():
        iteration(b0 + n_plain - 1, "plain")

    @pl.when(n_full > b0)
    def _():
        iteration(n_full - 1, "masked")

    iteration(n_full, None)
    values(n_full)


def _normalised(acc_ref):
    return acc_ref[0:HEAD_W, :] * (1.0 / acc_ref[HEAD_W:HEAD_W + 1, :])


def _first_needed_block(h, i, slope, qmax_ref, kmax_ref, pmin_ref, pmax_ref):
    nblk = SEQ // TK
    qm = qmax_ref[h * nblk + i]
    row_max_floor = -(qm * kmax_ref[h * nblk + i])

    def skippable(b):
        gap = jnp.maximum(jnp.maximum(pmin_ref[i] - pmax_ref[b], pmin_ref[b] - pmax_ref[i]), 0.0)
        return qm * kmax_ref[h * nblk + b] - slope * gap - row_max_floor < -ALIBI_SKIP_LOG2

    return lax.while_loop(lambda b: jnp.logical_and(b < i, skippable(b)), lambda b: b + 1, jnp.int32(0))


def _attn_a_kernel(slope_ref, qmax_ref, kmax_ref, pmin_ref, pmax_ref, q_ref, k_ref, vt_ref, pq_ref, pk_ref,
                   dl_ref, g_ref, o_ref, s1, s2, p1, p2, acc1, acc2, st, *, lam_init):
    h = pl.program_id(0)
    i = pl.program_id(1)
    q = q_ref[...]
    lane = lax.broadcasted_iota(jnp.int32, (TQ_A, HEAD_W), 1)
    zero = jnp.zeros_like(q)
    q1 = jnp.where(lane < A_DHALF, q, zero)
    q2 = jnp.where(lane >= A_DHALF, q, zero)
    slope = slope_ref[h]
    aq = pq_ref[...] * slope

    def bias_fn(off):
        bk = pk_ref[pl.ds(off, TK), :] * slope
        d = aq - jnp.concatenate([bk] * (TQ_A // LANES), axis=1)
        return pltpu.bitcast(pltpu.bitcast(d, jnp.uint32) | jnp.uint32(0x80000000), F32)

    b0 = _first_needed_block(h, i, slope, qmax_ref, kmax_ref, pmin_ref, pmax_ref)
    _online_softmax(i, TQ_A, (k_ref, k_ref), (vt_ref, vt_ref), (q1, q2), (s1, s2), (p1, p2),
                    (acc1, acc2), st, bias_fn, b0)

    dl = dl_ref[...]
    lam = (jnp.exp(jnp.sum(dl[0:1] * dl[1:2], axis=1, keepdims=True))
           - jnp.exp(jnp.sum(dl[2:3] * dl[3:4], axis=1, keepdims=True)) + lam_init)
    o = _normalised(acc1) - lam * _normalised(acc2)
    r = lax.rsqrt(jnp.mean(o * o, axis=0, keepdims=True) + RMS_EPS)
    g = jnp.concatenate([g_ref[...]] * (TQ_A // LANES), axis=1)
    o_ref[...] = (o * r * g * (1.0 - lam_init)).T.astype(BF16)


def _attn_a(slopes, qmax, kmax, pmin, pmax, qk, vt, pq, pk, dl, g_b, lam_init):
    s = qk.shape[0]
    assert TQ_A == TK
    smem = pl.BlockSpec(memory_space=pltpu.SMEM)
    return pl.pallas_call(
        functools.partial(_attn_a_kernel, lam_init=lam_init),
        out_shape=jax.ShapeDtypeStruct((s, A_WIDTH), BF16),
        grid=(N_HEADS, s // TQ_A),
        in_specs=[smem, smem, smem, smem, smem,
                  pl.BlockSpec((TQ_A, HEAD_W), lambda h, i: (i, h)),
                  pl.BlockSpec((s, HEAD_W), lambda h, i: (0, N_HEADS + h)),
                  pl.BlockSpec((s // TK, VT_ROWS, TK), lambda h, i: (0, h, 0)),
                  pl.BlockSpec((1, TQ_A), lambda h, i: (0, i)),
                  pl.BlockSpec((s, LANES), lambda h, i: (0, 0)),
                  pl.BlockSpec((4, A_DHALF), lambda h, i: (0, 0)),
                  pl.BlockSpec((HEAD_W, LANES), lambda h, i: (0, 0))],
        out_specs=pl.BlockSpec((TQ_A, HEAD_W), lambda h, i: (i, h)),
        scratch_shapes=[pltpu.VMEM((TK, TQ_A), F32), pltpu.VMEM((TK, TQ_A), F32),
                        pltpu.VMEM((TK, TQ_A), BF16), pltpu.VMEM((TK, TQ_A), BF16),
                        pltpu.VMEM((VT_ROWS, TQ_A), F32), pltpu.VMEM((VT_ROWS, TQ_A), F32),
                        pltpu.VMEM((2 * ST_ROWS, TQ_A), F32)],
        compiler_params=_cparams(("parallel", "parallel")),
        name="attn_a",
    )(slopes, qmax, kmax, pmin, pmax, qk, qk, vt, pq, pk, dl, g_b)


def _attn_c_kernel(qa_ref, qb_ref, ka_ref, kb_ref, vta_ref, vtb_ref, o_ref, sa, sb, pa, pb, acca, accb, st):
    i = pl.program_id(1)
    _online_softmax(i, TQ_C, (ka_ref, kb_ref), (vta_ref, vtb_ref), (qa_ref[...], qb_ref[...]),
                    (sa, sb), (pa, pb), (acca, accb), st, None)
    for x, acc in enumerate((acca, accb)):
        o_ref[:, x * HEAD_W:(x + 1) * HEAD_W] = _normalised(acc).T.astype(BF16)


def _attn_c(q, k, vt):
    s = q.shape[0]

    def head(x):
        return (pl.BlockSpec((TQ_C, HEAD_W), lambda h, i: (i, 2 * h + x)),
                pl.BlockSpec((s, HEAD_W), lambda h, i: (0, 2 * h + x)),
                pl.BlockSpec((s // TK, VT_ROWS, TK), lambda h, i: (0, 2 * h + x, 0)))

    (qa, ka, va), (qb, kb, vb) = head(0), head(1)
    return pl.pallas_call(
        _attn_c_kernel,
        out_shape=jax.ShapeDtypeStruct((s, C_WIDTH), BF16),
        grid=(N_HEADS // 2, s // TQ_C),
        in_specs=[qa, qb, ka, kb, va, vb],
        out_specs=pl.BlockSpec((TQ_C, 2 * HEAD_W), lambda h, i: (i, h)),
        scratch_shapes=[pltpu.VMEM((TK, TQ_C), F32), pltpu.VMEM((TK, TQ_C), F32),
                        pltpu.VMEM((TK, TQ_C), BF16), pltpu.VMEM((TK, TQ_C), BF16),
                        pltpu.VMEM((VT_ROWS, TQ_C), F32), pltpu.VMEM((VT_ROWS, TQ_C), F32),
                        pltpu.VMEM((2 * ST_ROWS, TQ_C), F32)],
        compiler_params=_cparams(("parallel", "parallel")),
        name="attn_c",
    )(q, q, k, k, vt, vt)


def _out_ln_kernel(oa_ref, ob_ref, oc_ref, wa_ref, wb_ref, wc_ref, h_ref, g_ref, b_ref,
                   y_ref, yb_ref):
    m = _dot(oa_ref[...], wa_ref[...]) + _dot(ob_ref[...], wb_ref[...]) + _dot(oc_ref[...], wc_ref[...])
    y = _layer_norm(ALPHA * h_ref[...] + m, g_ref[...], b_ref[...])
    y_ref[...] = y
    yb_ref[...] = y.astype(BF16)


def _out_ln(oa, ob, oc, wa, wb, wc, h, g, b):
    s, d = h.shape
    row = pl.BlockSpec((TS, d), lambda i: (i, 0))
    vec = pl.BlockSpec((1, d), lambda i: (0, 0))

    def rows(a):
        return pl.BlockSpec((TS, a.shape[1]), lambda i: (i, 0))

    def full(a):
        return pl.BlockSpec(a.shape, lambda i: (0, 0))

    return pl.pallas_call(
        _out_ln_kernel,
        out_shape=(jax.ShapeDtypeStruct((s, d), F32), jax.ShapeDtypeStruct((s, d), BF16)),
        grid=(s // TS,),
        in_specs=[rows(oa), rows(ob), rows(oc), full(wa), full(wb), full(wc), row, vec, vec],
        out_specs=(row, row),
        compiler_params=_cparams(("parallel",)),
        name="out_ln",
    )(oa, ob, oc, wa, wb, wc, h, g.reshape(1, d), b.reshape(1, d))


def _ffn_kernel(xb_ref, wg_ref, wu_ref, cw_ref, wd_ref, h_ref, g_ref, b_ref, y_ref, yb_ref,
                gbuf, carry, acc):
    i = pl.program_id(0)
    j = pl.program_id(1)

    @pl.when(i == 0)
    def _():
        carry[j] = jnp.zeros((HALO, TF), F32)

    @pl.when(j == 0)
    def _():
        acc[...] = jnp.zeros_like(acc)

    gbuf[pl.ds(0, HALO), :] = carry[j]
    gbuf[pl.ds(HALO, TS), :] = _dot(xb_ref[...], wg_ref[...])
    carry[j] = gbuf[pl.ds(TS, HALO), :]
    gate = _causal_conv3(gbuf, cw_ref, TS)
    act = (gate * (1.0 / (1.0 + jnp.exp(-gate))) * _dot(xb_ref[...], wu_ref[...])).astype(BF16)
    acc[...] += _dot(act, wd_ref[...])

    @pl.when(j == pl.num_programs(1) - 1)
    def _():
        y = _layer_norm(ALPHA * h_ref[...] + acc[...], g_ref[...], b_ref[...])
        y_ref[...] = y
        yb_ref[...] = y.astype(BF16)


def _cast_kernel(x_ref, o_ref):
    o_ref[...] = x_ref[...].astype(BF16)


def _to_bf16(w, rows):
    n, r, c = w.shape
    spec = pl.BlockSpec((None, rows, c), lambda a, i: (a, i, 0))
    return pl.pallas_call(
        _cast_kernel,
        out_shape=jax.ShapeDtypeStruct(w.shape, BF16),
        grid=(n, r // rows),
        in_specs=[spec],
        out_specs=spec,
        compiler_params=_cparams(("parallel", "parallel")),
        name="to_bf16",
    )(w)


def _ffn(layer, xb, wg, wu, conv_w, wd, h, g, b):
    s, d = h.shape
    nf = D_FF // TF
    row = pl.BlockSpec((TS, d), lambda i, j: (i, 0))
    vec = pl.BlockSpec((1, d), lambda i, j: (0, 0))
    return pl.pallas_call(
        _ffn_kernel,
        out_shape=(jax.ShapeDtypeStruct((s, d), F32), jax.ShapeDtypeStruct((s, d), BF16)),
        grid=(s // TS, nf),
        in_specs=[row,
                  pl.BlockSpec((None, d, TF), lambda i, j: (layer, 0, j)),
                  pl.BlockSpec((None, d, TF), lambda i, j: (layer, 0, j)),
                  pl.BlockSpec((3, TF), lambda i, j: (0, j)),
                  pl.BlockSpec((None, TF, d), lambda i, j: (layer, j, 0)),
                  row, vec, vec],
        out_specs=(row, row),
        scratch_shapes=[pltpu.VMEM((HALO + TS, TF), F32),
                        pltpu.VMEM((nf, HALO, TF), F32),
                        pltpu.VMEM((TS, d), F32)],
        compiler_params=_cparams(("arbitrary", "arbitrary")),
        name="ffn",
    )(xb, wg, wu, conv_w, wd, h, g.reshape(1, d), b.reshape(1, d))


def _pad_heads(w, width):
    kdim = w.shape[0]
    w = w.reshape(kdim, N_HEADS, width)
    return jnp.pad(w, ((0, 0), (0, 0), (0, HEAD_W - width))).reshape(kdim, N_HEADS * HEAD_W)


def _prep_layer(w_in, w_uq, w_ukv, w_o):
    aq, ak, av = 0, A_WIDTH, 2 * A_WIDTH
    bh = 3 * A_WIDTH
    cq = bh + 3 * B_WIDTH
    ckv = cq + C_Q_RANK
    ckr = ckv + C_KV_RANK
    kr_tile = jnp.pad(w_in[:, ckr:], ((0, 0), (C_NOPE, HEAD_W - C_NOPE - C_ROPE)))
    ukv = w_ukv.reshape(C_KV_RANK, N_HEADS, C_NOPE + HEAD_W)
    return dict(
        w_qk=w_in[:, aq:av].astype(BF16),
        w_vt=w_in[:, av:bh].T.astype(BF16),
        w_b=w_in[:, bh:cq].astype(BF16),
        w_c=jnp.concatenate([w_in[:, cq:ckr], kr_tile], axis=1).astype(BF16),
        w_uq=_pad_heads(w_uq, C_NOPE + C_ROPE).astype(BF16),
        w_uk=_pad_heads(ukv[:, :, :C_NOPE].reshape(C_KV_RANK, N_HEADS * C_NOPE), C_NOPE).astype(BF16),
        w_uvt=ukv[:, :, C_NOPE:].reshape(C_KV_RANK, C_WIDTH).T.astype(BF16),
        w_oa=w_o[:A_WIDTH].astype(BF16),
        w_ob=w_o[A_WIDTH:A_WIDTH + B_WIDTH].astype(BF16),
        w_oc=w_o[A_WIDTH + B_WIDTH:].astype(BF16),
    )


def kernel(x, positions, ln_in_g, ln_in_b, w_in, diff_lambda, diff_norm_g, conv_w, mla_q_norm_g, mla_kv_norm_g, w_uq, w_ukv, w_o, ln1_g, ln1_b, ffn_w_gate, ffn_w_up, ffn_conv_w, ffn_w_down, ln2_g, ln2_b):
    batch, s, d = x.shape
    assert (batch, s, d) == (1, SEQ, D_MODEL)
    pos = positions.reshape(s)
    pos_f = pos.astype(F32)
    pq = pos_f.reshape(1, s)
    pk = jnp.broadcast_to(pos_f.reshape(s, 1), (s, LANES))
    n_blk = s // TK
    pmin, pmax = pos_f.reshape(n_blk, TK).min(axis=1), pos_f.reshape(n_blk, TK).max(axis=1)
    heads = jnp.arange(N_HEADS, dtype=F32)
    slopes = (2.0 ** (-8.0 * (heads + 1.0) / N_HEADS)) * LOG2E
    qk_scale = jnp.concatenate([jnp.full((1, A_WIDTH), A_DHALF ** -0.5 * LOG2E, F32),
                                jnp.ones((1, A_WIDTH), F32)], axis=1)
    c_scale = (C_NOPE + C_ROPE) ** -0.5 * LOG2E

    wg_b, wu_b = _to_bf16(ffn_w_gate, 256), _to_bf16(ffn_w_up, 256)
    wd_b = _to_bf16(ffn_w_down, D_FF // 8)
    cos, sina, sinb = _rope_tables(pos)
    h, hb = _ln_in(x.reshape(s, d), ln_in_g, ln_in_b)
    for l in range(DEPTH):
        lam_init = 0.8 - 0.6 * math.exp(-0.3 * l)
        w = _prep_layer(w_in[l], w_uq[l], w_ukv[l], w_o[l])
        qk, sq_norms = _proj_qk(hb, w["w_qk"], qk_scale)
        norms = jnp.sqrt(sq_norms[:, :, 0, :2 * N_HEADS]).reshape(2, n_blk, N_HEADS, 2).max(-1) * NORM_SLACK
        qmax, kmax = norms[0].T.reshape(-1), norms[1].T.reshape(-1)
        vt_a = _proj_vt(hb, w["w_vt"])
        g_b = jnp.broadcast_to(diff_norm_g[l].reshape(HEAD_W, 1), (HEAD_W, LANES))
        o_a = _attn_a(slopes, qmax, kmax, pmin, pmax, qk, vt_a, pq, pk, diff_lambda[l], g_b, lam_init)
        o_b = _mix_b(hb, w["w_b"], conv_w[l])
        q_c, k_c, vt_c = _proj_c(hb, w["w_c"], mla_q_norm_g[l].reshape(1, -1), mla_kv_norm_g[l].reshape(1, -1),
                                 w["w_uq"], w["w_uk"], w["w_uvt"], cos, sina, sinb, c_scale)
        o_c = _attn_c(q_c, k_c, vt_c)
        h, hb = _out_ln(o_a, o_b, o_c, w["w_oa"], w["w_ob"], w["w_oc"], h, ln1_g[l], ln1_b[l])
        h, hb = _ffn(l, hb, wg_b, wu_b, ffn_conv_w[l], wd_b, h, ln2_g[l], ln2_b[l])
    return h.reshape(batch, s, d)
```

```python
import functools
import math

import jax
import jax.numpy as jnp
from jax import lax
from jax.experimental import pallas as pl
from jax.experimental.pallas import tpu as pltpu

F32 = jnp.float32
BF16 = jnp.bfloat16

D_MODEL = 2048
SEQ = 16384
DEPTH = 2
CHUNK = 64
N_HEADS = 6
HEAD_W = 128
BF16_ROWS = 16
VT_ROWS = HEAD_W + BF16_ROWS
A_DHALF = 64
A_WIDTH = N_HEADS * HEAD_W
B_WIDTH = 512
C_NOPE = 64
C_ROPE = 32
C_WIDTH = N_HEADS * HEAD_W
C_Q_RANK = 512
C_KV_RANK = 256
ROPE_THETA = 10000.0
D_FF = 5632
ALPHA = (2.0 * DEPTH) ** 0.25
LN_EPS = 1e-5
RMS_EPS = 1e-6
NEG = -1e30
LOG2E = math.log2(math.e)
ALIBI_SKIP_LOG2 = 160.0
NORM_SLACK = 1.02

LANES = 128
TS = 512
TQ_A = 512
TQ_C = 512
TK = 512
TF = 512
HALO = 8
VMEM_LIMIT = 56 * 1024 * 1024


def _cparams(sem):
    return pltpu.CompilerParams(dimension_semantics=sem, vmem_limit_bytes=VMEM_LIMIT)


def _layer_norm(x, g, b):
    mu = jnp.mean(x, axis=-1, keepdims=True)
    xc = x - mu
    var = jnp.mean(xc * xc, axis=-1, keepdims=True)
    return xc * lax.rsqrt(var + LN_EPS) * g + b


def _rms_norm(x, g):
    return x * lax.rsqrt(jnp.mean(x * x, axis=-1, keepdims=True) + RMS_EPS) * g


def _dot(a, b):
    return jnp.dot(a, b, preferred_element_type=F32)


def _dot_nt(a, b):
    return lax.dot_general(a, b, (((1,), (1,)), ((), ())), preferred_element_type=F32)


def _ln_in_kernel(x_ref, g_ref, b_ref, h_ref, hb_ref):
    h = _layer_norm(x_ref[...], g_ref[...], b_ref[...])
    h_ref[...] = h
    hb_ref[...] = h.astype(BF16)


def _ln_in(x, g, b):
    s, d = x.shape
    row = pl.BlockSpec((TS, d), lambda i: (i, 0))
    vec = pl.BlockSpec((1, d), lambda i: (0, 0))
    return pl.pallas_call(
        _ln_in_kernel,
        out_shape=(jax.ShapeDtypeStruct((s, d), F32), jax.ShapeDtypeStruct((s, d), BF16)),
        grid=(s // TS,),
        in_specs=[row, vec, vec],
        out_specs=(row, row),
        compiler_params=_cparams(("parallel",)),
        name="ln_in",
    )(x, g.reshape(1, d), b.reshape(1, d))


def _rope_tab_kernel(pos_ref, invf_ref, ma_ref, mb_ref, cos_ref, sina_ref, sinb_ref):
    ang = pos_ref[...].astype(F32) * invf_ref[...]
    sn = jnp.sin(ang)
    cos_ref[...] = jnp.cos(ang)
    sina_ref[...] = sn * ma_ref[...]
    sinb_ref[...] = sn * mb_ref[...]


def _rope_tables(positions):
    s = positions.shape[0]
    half = C_ROPE // 2
    inv_freq = ROPE_THETA ** (-jnp.arange(0, C_ROPE, 2, dtype=F32) / C_ROPE)
    zeros = jnp.zeros((LANES,), F32)
    invf = zeros.at[C_NOPE:C_NOPE + half].set(inv_freq).at[C_NOPE + half:C_NOPE + C_ROPE].set(inv_freq)
    mask_a = zeros.at[C_NOPE + half:C_NOPE + C_ROPE].set(1.0)
    mask_b = zeros.at[C_NOPE:C_NOPE + half].set(-1.0)
    tab = jax.ShapeDtypeStruct((s, LANES), F32)
    row = pl.BlockSpec((TS, LANES), lambda i: (i, 0))
    vec = pl.BlockSpec((1, LANES), lambda i: (0, 0))
    return pl.pallas_call(
        _rope_tab_kernel,
        out_shape=(tab, tab, tab),
        grid=(s // TS,),
        in_specs=[pl.BlockSpec((TS, 1), lambda i: (i, 0)), vec, vec, vec],
        out_specs=(row, row, row),
        compiler_params=_cparams(("parallel",)),
        name="rope_tables",
    )(positions.reshape(s, 1), invf.reshape(1, LANES), mask_a.reshape(1, LANES), mask_b.reshape(1, LANES))


def _rope(x, cos, sina, sinb):
    half = C_ROPE // 2
    return x * cos + pltpu.roll(x, half, 1) * sina + pltpu.roll(x, LANES - half, 1) * sinb


def _proj_qk_kernel(x_ref, w_ref, s_ref, e_ref, o_ref, n_ref):
    o = (_dot(x_ref[...], w_ref[...]) * s_ref[...]).astype(BF16)
    o_ref[...] = o
    of = o.astype(F32)
    seg = _dot((of * of).astype(BF16), e_ref[...])
    n_ref[...] = jnp.broadcast_to(jnp.max(seg, axis=0, keepdims=True), (8, LANES))


def _proj_qk(hb, w, col_scale):
    s, d = hb.shape
    n = w.shape[1]
    tn = A_WIDTH
    seg_of_col = jnp.arange(tn, dtype=jnp.int32) // A_DHALF
    e = (seg_of_col[:, None] == jnp.arange(LANES, dtype=jnp.int32)[None, :]).astype(BF16)
    return pl.pallas_call(
        _proj_qk_kernel,
        out_shape=(jax.ShapeDtypeStruct((s, n), BF16),
                   jax.ShapeDtypeStruct((n // tn, s // TS, 8, LANES), F32)),
        grid=(n // tn, s // TS),
        in_specs=[pl.BlockSpec((TS, d), lambda j, i: (i, 0)),
                  pl.BlockSpec((d, tn), lambda j, i: (0, j)),
                  pl.BlockSpec((1, tn), lambda j, i: (0, j)),
                  pl.BlockSpec((tn, LANES), lambda j, i: (0, 0))],
        out_specs=(pl.BlockSpec((TS, tn), lambda j, i: (i, j)),
                   pl.BlockSpec((None, None, 8, LANES), lambda j, i: (j, i, 0, 0))),
        compiler_params=_cparams(("parallel", "parallel")),
        name="proj_qk",
    )(hb, w, col_scale, e)


def _store_vt(vt_ref, vt):
    ones_row = (lax.broadcasted_iota(jnp.int32, (BF16_ROWS, TK), 0) == 0).astype(BF16)
    for h in range(N_HEADS):
        vt_ref[h * VT_ROWS:h * VT_ROWS + HEAD_W, :] = vt[h * HEAD_W:(h + 1) * HEAD_W, :].astype(BF16)
        vt_ref[h * VT_ROWS + HEAD_W:(h + 1) * VT_ROWS, :] = ones_row


def _proj_vt_kernel(x_ref, w_ref, o_ref):
    _store_vt(o_ref, _dot_nt(w_ref[...], x_ref[...]))


def _proj_vt(hb, w_t):
    s, d = hb.shape
    n = w_t.shape[0]
    return pl.pallas_call(
        _proj_vt_kernel,
        out_shape=jax.ShapeDtypeStruct((s // TK, N_HEADS * VT_ROWS, TK), BF16),
        grid=(s // TK,),
        in_specs=[pl.BlockSpec((TK, d), lambda i: (i, 0)),
                  pl.BlockSpec((n, d), lambda i: (0, 0))],
        out_specs=pl.BlockSpec((None, N_HEADS * VT_ROWS, TK), lambda i: (i, 0, 0)),
        compiler_params=_cparams(("parallel",)),
        name="proj_vt",
    )(hb, w_t)


def _causal_conv3(buf_ref, cw_ref, rows):
    return (cw_ref[0:1, :] * buf_ref[pl.ds(HALO - 2, rows), :]
            + cw_ref[1:2, :] * buf_ref[pl.ds(HALO - 1, rows), :]
            + cw_ref[2:3, :] * buf_ref[pl.ds(HALO, rows), :])


def _mix_b_kernel(x_ref, w_ref, cw_ref, o_ref, ubuf):
    i = pl.program_id(0)

    @pl.when(i == 0)
    def _():
        ubuf[pl.ds(0, HALO), :] = jnp.zeros((HALO, B_WIDTH), F32)

    p = _dot(x_ref[...], w_ref[...])
    ubuf[pl.ds(HALO, TS), :] = p[:, 2 * B_WIDTH:] * p[:, :B_WIDTH]
    y = _causal_conv3(ubuf, cw_ref, TS)
    o_ref[...] = (p[:, B_WIDTH:2 * B_WIDTH] * y).astype(BF16)
    ubuf[pl.ds(0, HALO), :] = ubuf[pl.ds(TS, HALO), :]


def _mix_b(hb, w, conv_w):
    s, d = hb.shape
    return pl.pallas_call(
        _mix_b_kernel,
        out_shape=jax.ShapeDtypeStruct((s, B_WIDTH), BF16),
        grid=(s // TS,),
        in_specs=[pl.BlockSpec((TS, d), lambda i: (i, 0)),
                  pl.BlockSpec((d, 3 * B_WIDTH), lambda i: (0, 0)),
                  pl.BlockSpec((3, B_WIDTH), lambda i: (0, 0))],
        out_specs=pl.BlockSpec((TS, B_WIDTH), lambda i: (i, 0)),
        scratch_shapes=[pltpu.VMEM((HALO + TS, B_WIDTH), F32)],
        compiler_params=_cparams(("arbitrary",)),
        name="mix_b",
    )(hb, w, conv_w)


def _proj_c_kernel(x_ref, wc_ref, gq_ref, gkv_ref, wuq_ref, wuk_ref, wuvt_ref,
                   cos_ref, sina_ref, sinb_ref, q_ref, k_ref, vt_ref, *, q_scale):
    p = _dot(x_ref[...], wc_ref[...])
    cos, sina, sinb = cos_ref[...], sina_ref[...], sinb_ref[...]
    cq = _rms_norm(p[:, :C_Q_RANK], gq_ref[...]).astype(BF16)
    ckv = _rms_norm(p[:, C_Q_RANK:C_Q_RANK + C_KV_RANK], gkv_ref[...]).astype(BF16)
    k_pe = _rope(p[:, C_Q_RANK + C_KV_RANK:], cos, sina, sinb)
    qc = _dot(cq, wuq_ref[...])
    kc = _dot(ckv, wuk_ref[...])
    for h in range(N_HEADS):
        sl = slice(h * HEAD_W, (h + 1) * HEAD_W)
        q_ref[:, sl] = (_rope(qc[:, sl], cos, sina, sinb) * q_scale).astype(BF16)
        k_ref[:, sl] = (kc[:, sl] + k_pe).astype(BF16)
    _store_vt(vt_ref, _dot_nt(wuvt_ref[...], ckv))


def _proj_c(hb, wc, gq, gkv, wuq, wuk, wuvt, cos, sina, sinb, q_scale):
    s, d = hb.shape
    nc = wc.shape[1]
    row = pl.BlockSpec((TK, C_WIDTH), lambda i: (i, 0))
    tab = pl.BlockSpec((TK, LANES), lambda i: (i, 0))

    def full(a):
        return pl.BlockSpec(a.shape, lambda i: (0,) * a.ndim)

    return pl.pallas_call(
        functools.partial(_proj_c_kernel, q_scale=q_scale),
        out_shape=(jax.ShapeDtypeStruct((s, C_WIDTH), BF16),
                   jax.ShapeDtypeStruct((s, C_WIDTH), BF16),
                   jax.ShapeDtypeStruct((s // TK, N_HEADS * VT_ROWS, TK), BF16)),
        grid=(s // TK,),
        in_specs=[pl.BlockSpec((TK, d), lambda i: (i, 0)), full(wc), full(gq), full(gkv),
                  full(wuq), full(wuk), full(wuvt), tab, tab, tab],
        out_specs=(row, row, pl.BlockSpec((None, N_HEADS * VT_ROWS, TK), lambda i: (i, 0, 0))),
        compiler_params=_cparams(("parallel",)),
        name="proj_c",
    )(hb, wc, gq, gkv, wuq, wuk, wuvt, cos, sina, sinb)


ST_M, ST_ALPHA, ST_BM, ST_ROWS = 0, 1, 2, 4


def _stat(st_ref, x, row):
    r = x * ST_ROWS + row
    return st_ref.at[r:r + 1, :]


def _online_softmax(i, tq, k_refs, vt_refs, qs, s_refs, p_refs, acc_refs, st_ref, bias_fn, b0=0):
    n_maps = len(qs)
    n_full = (i * tq) // TK

    def scores(b, masked):
        off = pl.multiple_of(b * TK, TK)
        ks = {id(r): r[pl.ds(off, TK), :] for r in k_refs}
        bias = None if bias_fn is None else bias_fn(off)
        if masked:
            kidx = b * TK + lax.broadcasted_iota(jnp.int32, (TK, tq), 0)
            qidx = i * tq + lax.broadcasted_iota(jnp.int32, (TK, tq), 1)
            mask = (kidx // CHUNK) <= (qidx // CHUNK)
            if bias is not None:
                bias = jnp.where(mask, bias, NEG)
        for x in range(n_maps):
            s = _dot_nt(ks[id(k_refs[x])], qs[x])
            if bias is not None:
                s = s + bias
            elif masked:
                s = jnp.where(mask, s, NEG)
            s_refs[x][...] = s
            _stat(st_ref, x, ST_BM)[...] = jnp.max(s, axis=0, keepdims=True)

    def numerators(x):
        m_old = _stat(st_ref, x, ST_M)[...]
        m_new = jnp.maximum(m_old, _stat(st_ref, x, ST_BM)[...])
        _stat(st_ref, x, ST_M)[...] = m_new
        _stat(st_ref, x, ST_ALPHA)[...] = jnp.exp2(m_old - m_new)
        p_refs[x][...] = jnp.exp2(s_refs[x][...] - m_new).astype(BF16)

    def values(b):
        vts = {id(r): r[jnp.maximum(b, 0)] for r in vt_refs}
        for x in range(n_maps):
            pv = _dot(vts[id(vt_refs[x])], p_refs[x][...])
            acc_refs[x][...] = acc_refs[x][...] * _stat(st_ref, x, ST_ALPHA)[...] + pv

    def iteration(b, next_scores):
        values(b - 1)
        for x in range(n_maps):
            numerators(x)
        if next_scores is not None:
            scores(b + 1, masked=next_scores == "masked")

    for x in range(n_maps):
        p_refs[x][...] = jnp.zeros_like(p_refs[x])
        acc_refs[x][...] = jnp.zeros_like(acc_refs[x])
        _stat(st_ref, x, ST_M)[...] = jnp.full((1, tq), NEG, F32)
        _stat(st_ref, x, ST_ALPHA)[...] = jnp.ones((1, tq), F32)
    scores(b0, masked=True)

    n_plain = jnp.maximum(n_full - b0 - 1, 0)

    def pair(t, carry):
        iteration(b0 + 2 * t, "plain")
        iteration(b0 + 2 * t + 1, "plain")
        return carry

    lax.fori_loop(0, n_plain // 2, pair, 0)

    @pl.when(n_plain % 2 == 1)
    def _():
        iteration(b0 + n_plain - 1, "plain")

    @pl.when(n_full > b0)
    def _():
        iteration(n_full - 1, "masked")

    iteration(n_full, None)
    values(n_full)


def _normalised(acc_ref):
    return acc_ref[0:HEAD_W, :] * (1.0 / acc_ref[HEAD_W:HEAD_W + 1, :])


def _first_needed_block(h, i, slope, qmax_ref, kmax_ref, pmin_ref, pmax_ref):
    nblk = SEQ // TK
    qm = qmax_ref[h * nblk + i]
    row_max_floor = -(qm * kmax_ref[h * nblk + i])

    def skippable(b):
        gap = jnp.maximum(jnp.maximum(pmin_ref[i] - pmax_ref[b], pmin_ref[b] - pmax_ref[i]), 0.0)
        return qm * kmax_ref[h * nblk + b] - slope * gap - row_max_floor < -ALIBI_SKIP_LOG2

    return lax.while_loop(lambda b: jnp.logical_and(b < i, skippable(b)), lambda b: b + 1, jnp.int32(0))


def _attn_a_kernel(slope_ref, qmax_ref, kmax_ref, pmin_ref, pmax_ref, q_ref, k_ref, vt_ref, pq_ref, pk_ref,
                   dl_ref, g_ref, o_ref, s1, s2, p1, p2, acc1, acc2, st, *, lam_init):
    h = pl.program_id(0)
    i = pl.program_id(1)
    q = q_ref[...]
    lane = lax.broadcasted_iota(jnp.int32, (TQ_A, HEAD_W), 1)
    zero = jnp.zeros_like(q)
    q1 = jnp.where(lane < A_DHALF, q, zero)
    q2 = jnp.where(lane >= A_DHALF, q, zero)
    slope = slope_ref[h]
    aq = pq_ref[...] * slope

    def bias_fn(off):
        bk = pk_ref[pl.ds(off, TK), :] * slope
        d = aq - jnp.concatenate([bk] * (TQ_A // LANES), axis=1)
        return pltpu.bitcast(pltpu.bitcast(d, jnp.uint32) | jnp.uint32(0x80000000), F32)

    b0 = _first_needed_block(h, i, slope, qmax_ref, kmax_ref, pmin_ref, pmax_ref)
    _online_softmax(i, TQ_A, (k_ref, k_ref), (vt_ref, vt_ref), (q1, q2), (s1, s2), (p1, p2),
                    (acc1, acc2), st, bias_fn, b0)

    dl = dl_ref[...]
    lam = (jnp.exp(jnp.sum(dl[0:1] * dl[1:2], axis=1, keepdims=True))
           - jnp.exp(jnp.sum(dl[2:3] * dl[3:4], axis=1, keepdims=True)) + lam_init)
    o = _normalised(acc1) - lam * _normalised(acc2)
    r = lax.rsqrt(jnp.mean(o * o, axis=0, keepdims=True) + RMS_EPS)
    g = jnp.concatenate([g_ref[...]] * (TQ_A // LANES), axis=1)
    o_ref[...] = (o * r * g * (1.0 - lam_init)).T.astype(BF16)


def _attn_a(slopes, qmax, kmax, pmin, pmax, qk, vt, pq, pk, dl, g_b, lam_init):
    s = qk.shape[0]
    assert TQ_A == TK
    smem = pl.BlockSpec(memory_space=pltpu.SMEM)
    return pl.pallas_call(
        functools.partial(_attn_a_kernel, lam_init=lam_init),
        out_shape=jax.ShapeDtypeStruct((s, A_WIDTH), BF16),
        grid=(N_HEADS, s // TQ_A),
        in_specs=[smem, smem, smem, smem, smem,
                  pl.BlockSpec((TQ_A, HEAD_W), lambda h, i: (i, h)),
                  pl.BlockSpec((s, HEAD_W), lambda h, i: (0, N_HEADS + h)),
                  pl.BlockSpec((s // TK, VT_ROWS, TK), lambda h, i: (0, h, 0)),
                  pl.BlockSpec((1, TQ_A), lambda h, i: (0, i)),
                  pl.BlockSpec((s, LANES), lambda h, i: (0, 0)),
                  pl.BlockSpec((4, A_DHALF), lambda h, i: (0, 0)),
                  pl.BlockSpec((HEAD_W, LANES), lambda h, i: (0, 0))],
        out_specs=pl.BlockSpec((TQ_A, HEAD_W), lambda h, i: (i, h)),
        scratch_shapes=[pltpu.VMEM((TK, TQ_A), F32), pltpu.VMEM((TK, TQ_A), F32),
                        pltpu.VMEM((TK, TQ_A), BF16), pltpu.VMEM((TK, TQ_A), BF16),
                        pltpu.VMEM((VT_ROWS, TQ_A), F32), pltpu.VMEM((VT_ROWS, TQ_A), F32),
                        pltpu.VMEM((2 * ST_ROWS, TQ_A), F32)],
        compiler_params=_cparams(("parallel", "parallel")),
        name="attn_a",
    )(slopes, qmax, kmax, pmin, pmax, qk, qk, vt, pq, pk, dl, g_b)


def _attn_c_kernel(qa_ref, qb_ref, ka_ref, kb_ref, vta_ref, vtb_ref, o_ref, sa, sb, pa, pb, acca, accb, st):
    i = pl.program_id(1)
    _online_softmax(i, TQ_C, (ka_ref, kb_ref), (vta_ref, vtb_ref), (qa_ref[...], qb_ref[...]),
                    (sa, sb), (pa, pb), (acca, accb), st, None)
    for x, acc in enumerate((acca, accb)):
        o_ref[:, x * HEAD_W:(x + 1) * HEAD_W] = _normalised(acc).T.astype(BF16)


def _attn_c(q, k, vt):
    s = q.shape[0]

    def head(x):
        return (pl.BlockSpec((TQ_C, HEAD_W), lambda h, i: (i, 2 * h + x)),
                pl.BlockSpec((s, HEAD_W), lambda h, i: (0, 2 * h + x)),
                pl.BlockSpec((s // TK, VT_ROWS, TK), lambda h, i: (0, 2 * h + x, 0)))

    (qa, ka, va), (qb, kb, vb) = head(0), head(1)
    return pl.pallas_call(
        _attn_c_kernel,
        out_shape=jax.ShapeDtypeStruct((s, C_WIDTH), BF16),
        grid=(N_HEADS // 2, s // TQ_C),
        in_specs=[qa, qb, ka, kb, va, vb],
        out_specs=pl.BlockSpec((TQ_C, 2 * HEAD_W), lambda h, i: (i, h)),
        scratch_shapes=[pltpu.VMEM((TK, TQ_C), F32), pltpu.VMEM((TK, TQ_C), F32),
                        pltpu.VMEM((TK, TQ_C), BF16), pltpu.VMEM((TK, TQ_C), BF16),
                        pltpu.VMEM((VT_ROWS, TQ_C), F32), pltpu.VMEM((VT_ROWS, TQ_C), F32),
                        pltpu.VMEM((2 * ST_ROWS, TQ_C), F32)],
        compiler_params=_cparams(("parallel", "parallel")),
        name="attn_c",
    )(q, q, k, k, vt, vt)


def _out_ln_kernel(oa_ref, ob_ref, oc_ref, wa_ref, wb_ref, wc_ref, h_ref, g_ref, b_ref,
                   y_ref, yb_ref):
    m = _dot(oa_ref[...], wa_ref[...]) + _dot(ob_ref[...], wb_ref[...]) + _dot(oc_ref[...], wc_ref[...])
    y = _layer_norm(ALPHA * h_ref[...] + m, g_ref[...], b_ref[...])
    y_ref[...] = y
    yb_ref[...] = y.astype(BF16)


def _out_ln(oa, ob, oc, wa, wb, wc, h, g, b):
    s, d = h.shape
    row = pl.BlockSpec((TS, d), lambda i: (i, 0))
    vec = pl.BlockSpec((1, d), lambda i: (0, 0))

    def rows(a):
        return pl.BlockSpec((TS, a.shape[1]), lambda i: (i, 0))

    def full(a):
        return pl.BlockSpec(a.shape, lambda i: (0, 0))

    return pl.pallas_call(
        _out_ln_kernel,
        out_shape=(jax.ShapeDtypeStruct((s, d), F32), jax.ShapeDtypeStruct((s, d), BF16)),
        grid=(s // TS,),
        in_specs=[rows(oa), rows(ob), rows(oc), full(wa), full(wb), full(wc), row, vec, vec],
        out_specs=(row, row),
        compiler_params=_cparams(("parallel",)),
        name="out_ln",
    )(oa, ob, oc, wa, wb, wc, h, g.reshape(1, d), b.reshape(1, d))


def _ffn_kernel(xb_ref, wg_ref, wu_ref, cw_ref, wd_ref, h_ref, g_ref, b_ref, y_ref, yb_ref,
                gbuf, carry, acc):
    i = pl.program_id(0)
    j = pl.program_id(1)

    @pl.when(i == 0)
    def _():
        carry[j] = jnp.zeros((HALO, TF), F32)

    @pl.when(j == 0)
    def _():
        acc[...] = jnp.zeros_like(acc)

    gbuf[pl.ds(0, HALO), :] = carry[j]
    gbuf[pl.ds(HALO, TS), :] = _dot(xb_ref[...], wg_ref[...])
    carry[j] = gbuf[pl.ds(TS, HALO), :]
    gate = _causal_conv3(gbuf, cw_ref, TS)
    act = (gate * (1.0 / (1.0 + jnp.exp(-gate))) * _dot(xb_ref[...], wu_ref[...])).astype(BF16)
    acc[...] += _dot(act, wd_ref[...])

    @pl.when(j == pl.num_programs(1) - 1)
    def _():
        y = _layer_norm(ALPHA * h_ref[...] + acc[...], g_ref[...], b_ref[...])
        y_ref[...] = y
        yb_ref[...] = y.astype(BF16)


def _cast_kernel(x_ref, o_ref):
    o_ref[...] = x_ref[...].astype(BF16)


def _to_bf16(w, rows):
    n, r, c = w.shape
    spec = pl.BlockSpec((None, rows, c), lambda a, i: (a, i, 0))
    return pl.pallas_call(
        _cast_kernel,
        out_shape=jax.ShapeDtypeStruct(w.shape, BF16),
        grid=(n, r // rows),
        in_specs=[spec],
        out_specs=spec,
        compiler_params=_cparams(("parallel", "parallel")),
        name="to_bf16",
    )(w)


def _ffn(layer, xb, wg, wu, conv_w, wd, h, g, b):
    s, d = h.shape
    nf = D_FF // TF
    row = pl.BlockSpec((TS, d), lambda i, j: (i, 0))
    vec = pl.BlockSpec((1, d), lambda i, j: (0, 0))
    return pl.pallas_call(
        _ffn_kernel,
        out_shape=(jax.ShapeDtypeStruct((s, d), F32), jax.ShapeDtypeStruct((s, d), BF16)),
        grid=(s // TS, nf),
        in_specs=[row,
                  pl.BlockSpec((None, d, TF), lambda i, j: (layer, 0, j)),
                  pl.BlockSpec((None, d, TF), lambda i, j: (layer, 0, j)),
                  pl.BlockSpec((3, TF), lambda i, j: (0, j)),
                  pl.BlockSpec((None, TF, d), lambda i, j: (layer, j, 0)),
                  row, vec, vec],
        out_specs=(row, row),
        scratch_shapes=[pltpu.VMEM((HALO + TS, TF), F32),
                        pltpu.VMEM((nf, HALO, TF), F32),
                        pltpu.VMEM((TS, d), F32)],
        compiler_params=_cparams(("arbitrary", "arbitrary")),
        name="ffn",
    )(xb, wg, wu, conv_w, wd, h, g.reshape(1, d), b.reshape(1, d))


def _pad_heads(w, width):
    kdim = w.shape[0]
    w = w.reshape(kdim, N_HEADS, width)
    return jnp.pad(w, ((0, 0), (0, 0), (0, HEAD_W - width))).reshape(kdim, N_HEADS * HEAD_W)


def _prep_layer(w_in, w_uq, w_ukv, w_o):
    aq, ak, av = 0, A_WIDTH, 2 * A_WIDTH
    bh = 3 * A_WIDTH
    cq = bh + 3 * B_WIDTH
    ckv = cq + C_Q_RANK
    ckr = ckv + C_KV_RANK
    kr_tile = jnp.pad(w_in[:, ckr:], ((0, 0), (C_NOPE, HEAD_W - C_NOPE - C_ROPE)))
    ukv = w_ukv.reshape(C_KV_RANK, N_HEADS, C_NOPE + HEAD_W)
    return dict(
        w_qk=w_in[:, aq:av].astype(BF16),
        w_vt=w_in[:, av:bh].T.astype(BF16),
        w_b=w_in[:, bh:cq].astype(BF16),
        w_c=jnp.concatenate([w_in[:, cq:ckr], kr_tile], axis=1).astype(BF16),
        w_uq=_pad_heads(w_uq, C_NOPE + C_ROPE).astype(BF16),
        w_uk=_pad_heads(ukv[:, :, :C_NOPE].reshape(C_KV_RANK, N_HEADS * C_NOPE), C_NOPE).astype(BF16),
        w_uvt=ukv[:, :, C_NOPE:].reshape(C_KV_RANK, C_WIDTH).T.astype(BF16),
        w_oa=w_o[:A_WIDTH].astype(BF16),
        w_ob=w_o[A_WIDTH:A_WIDTH + B_WIDTH].astype(BF16),
        w_oc=w_o[A_WIDTH + B_WIDTH:].astype(BF16),
    )


def kernel(x, positions, ln_in_g, ln_in_b, w_in, diff_lambda, diff_norm_g, conv_w, mla_q_norm_g, mla_kv_norm_g, w_uq, w_ukv, w_o, ln1_g, ln1_b, ffn_w_gate, ffn_w_up, ffn_conv_w, ffn_w_down, ln2_g, ln2_b):
    batch, s, d = x.shape
    assert (batch, s, d) == (1, SEQ, D_MODEL)
    pos = positions.reshape(s)
    pos_f = pos.astype(F32)
    pq = pos_f.reshape(1, s)
    pk = jnp.broadcast_to(pos_f.reshape(s, 1), (s, LANES))
    n_blk = s // TK
    pmin, pmax = pos_f.reshape(n_blk, TK).min(axis=1), pos_f.reshape(n_blk, TK).max(axis=1)
    heads = jnp.arange(N_HEADS, dtype=F32)
    slopes = (2.0 ** (-8.0 * (heads + 1.0) / N_HEADS)) * LOG2E
    qk_scale = jnp.concatenate([jnp.full((1, A_WIDTH), A_DHALF ** -0.5 * LOG2E, F32),
                                jnp.ones((1, A_WIDTH), F32)], axis=1)
    c_scale = (C_NOPE + C_ROPE) ** -0.5 * LOG2E

    wg_b, wu_b = _to_bf16(ffn_w_gate, 256), _to_bf16(ffn_w_up, 256)
    wd_b = _to_bf16(ffn_w_down, D_FF // 8)
    cos, sina, sinb = _rope_tables(pos)
    h, hb = _ln_in(x.reshape(s, d), ln_in_g, ln_in_b)
    for l in range(DEPTH):
        lam_init = 0.8 - 0.6 * math.exp(-0.3 * l)
        w = _prep_layer(w_in[l], w_uq[l], w_ukv[l], w_o[l])
        qk, sq_norms = _proj_qk(hb, w["w_qk"], qk_scale)
        norms = jnp.sqrt(sq_norms[:, :, 0, :2 * N_HEADS]).reshape(2, n_blk, N_HEADS, 2).max(-1) * NORM_SLACK
        qmax, kmax = norms[0].T.reshape(-1), norms[1].T.reshape(-1)
        vt_a = _proj_vt(hb, w["w_vt"])
        g_b = jnp.broadcast_to(diff_norm_g[l].reshape(HEAD_W, 1), (HEAD_W, LANES))
        o_a = _attn_a(slopes, qmax, kmax, pmin, pmax, qk, vt_a, pq, pk, diff_lambda[l], g_b, lam_init)
        o_b = _mix_b(hb, w["w_b"], conv_w[l])
        q_c, k_c, vt_c = _proj_c(hb, w["w_c"], mla_q_norm_g[l].reshape(1, -1), mla_kv_norm_g[l].reshape(1, -1),
                                 w["w_uq"], w["w_uk"], w["w_uvt"], cos, sina, sinb, c_scale)
        o_c = _attn_c(q_c, k_c, vt_c)
        h, hb = _out_ln(o_a, o_b, o_c, w["w_oa"], w["w_ob"], w["w_oc"], h, ln1_g[l], ln1_b[l])
        h, hb = _ffn(l, hb, wg_b, wu_b, ffn_conv_w[l], wd_b, h, ln2_g[l], ln2_b[l])
    return h.reshape(batch, s, d)
```
